```python
import math
import jax, jax.numpy as jnp
from jax import lax
import numpy as np

D_MODEL = 1024
BATCH = 8
SEQ = 2048
DEPTH = 2
DEC_BATCH = 32
DEC_SEQ = 4
PAST_LEN = 8192
PAGE_SIZE = 128

HEAD_DIM = 64
MOBA_HEADS = 4
MOBA_BLOCK = 256
MOBA_TOPK = 3
DIFF_HEADS = 4
DIFF_HD = HEAD_DIM
DIFF_VD = 2 * HEAD_DIM
SB_HEADS = 4
MOBA_W = MOBA_HEADS * HEAD_DIM
DIFF_QK_W = DIFF_HEADS * 2 * DIFF_HD
DIFF_V_W = DIFF_HEADS * DIFF_VD
SB_W = SB_HEADS * HEAD_DIM
QK_W = MOBA_W + DIFF_QK_W + SB_W
V_W = MOBA_W + DIFF_V_W + SB_W
N_BRANCH = 3
N_ALIBI = MOBA_HEADS + DIFF_HEADS
D_FF = 2816
CONV_W = 3
PLE_DIM = 256
Q_BLOCK = 128
MOBA_Q_BLOCK = 32
LN_EPS = 1e-5
DEEPNORM_ALPHA = (2.0 * DEPTH) ** 0.25
DEEPNORM_BETA = (8.0 * DEPTH) ** -0.25

kernel_name = 'hybrid_moba_diff_stickbreak_decoder_step'


def _layer_norm(x, g, b):
    xf = x.astype(jnp.float32)
    mu = jnp.mean(xf, -1, keepdims=True)
    var = jnp.mean(jnp.square(xf - mu), -1, keepdims=True)
    return ((xf - mu) * lax.rsqrt(var + LN_EPS) * g + b).astype(x.dtype)


def _rms_norm(x, g):
    xf = x.astype(jnp.float32)
    return (xf * lax.rsqrt(jnp.mean(jnp.square(xf), -1, keepdims=True) + LN_EPS) * g).astype(x.dtype)


def _alibi_slopes():
    return jnp.exp2(-8.0 * jnp.arange(1, N_ALIBI + 1, dtype=jnp.float32) / N_ALIBI)


def _sweep_queries(fn, q, q_pos, block):
    t = q.shape[1]
    if t <= block or t % block:
        return fn(q, q_pos)
    n = t // block
    qb = jnp.moveaxis(q.reshape((q.shape[0], n, block) + q.shape[2:]), 1, 0)
    out = lax.map(lambda a: fn(a[0], a[1]), (qb, q_pos.reshape(n, block)))
    out = jnp.moveaxis(out, 0, 1)
    return out.reshape((out.shape[0], t) + out.shape[3:])


def _moba_attention(q, k, v, q_pos, slopes):
    bsz, length, n_heads, hd = k.shape
    nb = -(-length // MOBA_BLOCK)
    pad = nb * MOBA_BLOCK - length
    if pad:
        k = jnp.pad(k, ((0, 0), (0, pad), (0, 0), (0, 0)))
        v = jnp.pad(v, ((0, 0), (0, pad), (0, 0), (0, 0)))
    kb = k.reshape(bsz, nb, MOBA_BLOCK, n_heads, hd)
    vb = v.reshape(bsz, nb, MOBA_BLOCK, n_heads, hd)
    k_mean = jnp.mean(kb.astype(jnp.float32), axis=2)
    n_sel = min(MOBA_TOPK, nb)
    scale = hd ** -0.5
    bi = jnp.arange(bsz)[:, None, None, None]
    hi = jnp.arange(n_heads)[None, :, None, None]
    offs = jnp.arange(MOBA_BLOCK)

    def attend(qc, pc):
        own = pc // MOBA_BLOCK
        gate = jnp.einsum('bqhd,bnhd->bhqn', qc.astype(jnp.float32), k_mean)
        full_past = jnp.arange(nb)[None, :] < own[:, None]
        gate = jnp.where(full_past, gate, -jnp.inf)
        _, sel = lax.top_k(gate, n_sel)
        sel_ok = sel < own[:, None]
        own_b = jnp.broadcast_to(own[:, None], sel.shape[:3] + (1,))
        idx = jnp.concatenate([sel, own_b], -1)
        ok = jnp.concatenate([sel_ok, jnp.ones(own_b.shape, dtype=bool)], -1)
        kg = kb[bi, idx, :, hi]
        vg = vb[bi, idx, :, hi]
        key_pos = idx[..., None] * MOBA_BLOCK + offs
        dist = (pc[:, None, None] - key_pos).astype(jnp.float32)
        valid = ok[..., None] & (dist >= 0)
        logits = jnp.einsum('bqhd,bhqjsd->bhqjs', qc, kg).astype(jnp.float32) * scale
        logits = logits - slopes[None, :, None, None, None] * dist
        logits = jnp.where(valid, logits, -jnp.inf)
        shp = logits.shape
        p = jax.nn.softmax(logits.reshape(shp[:3] + (-1,)), axis=-1).reshape(shp)
        return jnp.einsum('bhqjs,bhqjsd->bqhd', p.astype(vg.dtype), vg)

    return _sweep_queries(attend, q, q_pos, MOBA_Q_BLOCK)


def _diff_attention(q, k, v, q_pos, slopes, lam):
    length = k.shape[1]
    scale = q.shape[-1] ** -0.5
    kpos = jnp.arange(length)

    def attend(qc, pc):
        dist = (pc[:, None] - kpos[None, :]).astype(jnp.float32)
        s = jnp.einsum('bqhcd,bkhcd->bhcqk', qc, k).astype(jnp.float32) * scale
        s = s - slopes[None, :, None, None, None] * dist
        s = jnp.where(dist >= 0, s, -jnp.inf)
        p = jax.nn.softmax(s, axis=-1)
        w = p[:, :, 0] - lam * p[:, :, 1]
        return jnp.einsum('bhqk,bkhd->bqhd', w.astype(v.dtype), v)

    return _sweep_queries(attend, q, q_pos, Q_BLOCK)


def _stick_breaking_attention(q, k, v, q_pos):
    length = k.shape[1]
    scale = q.shape[-1] ** -0.5
    kpos = jnp.arange(length)

    def attend(qc, pc):
        before = kpos[None, :] < pc[:, None]
        z = jnp.einsum('bqhd,bkhd->bhqk', qc, k).astype(jnp.float32) * scale
        log_keep = jnp.where(before, jax.nn.log_sigmoid(-z), 0.0)
        later = lax.cumsum(log_keep, axis=3, reverse=True) - log_keep
        a = jnp.where(before, jnp.exp(jax.nn.log_sigmoid(z) + later), 0.0)
        return jnp.einsum('bhqk,bkhd->bqhd', a.astype(v.dtype), v)

    return _sweep_queries(attend, q, q_pos, Q_BLOCK)


def _trunk_layer(i, x, ple, k_past, v_past, conv_past, w_in, w_gate, w_branch, w_out,
                 diff_lambda, diff_subln_g, ln1_g, ln1_b, w_up, conv_w, conv_b, w_down,
                 ln2_g, ln2_b, w_ple, w_ple_gate):
    bsz, t, _ = x.shape
    past = 0 if k_past is None else k_past.shape[1]
    q_pos = past + jnp.arange(t)
    q, k_new, v_new = jnp.split(x @ w_in, [QK_W, 2 * QK_W], axis=-1)
    k = k_new if k_past is None else jnp.concatenate([k_past, k_new], 1)
    v = v_new if v_past is None else jnp.concatenate([v_past, v_new], 1)
    length = k.shape[1]
    slopes = _alibi_slopes()
    qm, qd, qs = jnp.split(q, [MOBA_W, MOBA_W + DIFF_QK_W], -1)
    km, kd, ks = jnp.split(k, [MOBA_W, MOBA_W + DIFF_QK_W], -1)
    vm, vd, vs = jnp.split(v, [MOBA_W, MOBA_W + DIFF_V_W], -1)

    o_m = _moba_attention(qm.reshape(bsz, t, MOBA_HEADS, HEAD_DIM),
                          km.reshape(bsz, length, MOBA_HEADS, HEAD_DIM),
                          vm.reshape(bsz, length, MOBA_HEADS, HEAD_DIM), q_pos, slopes[0::2])

    lam_init = 0.8 - 0.6 * math.exp(-0.3 * i)
    lam = (jnp.exp(jnp.sum((diff_lambda[0] * diff_lambda[1]).astype(jnp.float32)))
           - jnp.exp(jnp.sum((diff_lambda[2] * diff_lambda[3]).astype(jnp.float32))) + lam_init)
    o_d = _diff_attention(qd.reshape(bsz, t, DIFF_HEADS, 2, DIFF_HD),
                          kd.reshape(bsz, length, DIFF_HEADS, 2, DIFF_HD),
                          vd.reshape(bsz, length, DIFF_HEADS, DIFF_VD), q_pos, slopes[1::2], lam)
    o_d = _rms_norm(o_d, diff_subln_g) * (1.0 - lam_init)

    o_s = _stick_breaking_attention(qs.reshape(bsz, t, SB_HEADS, HEAD_DIM),
                                    ks.reshape(bsz, length, SB_HEADS, HEAD_DIM),
                                    vs.reshape(bsz, length, SB_HEADS, HEAD_DIM), q_pos)

    wb_m, wb_d, wb_s = jnp.split(w_branch, [MOBA_W, MOBA_W + DIFF_V_W], 0)
    gates = jax.nn.sigmoid(x @ w_gate).reshape(bsz, t, N_BRANCH, D_MODEL)
    merged = (gates[:, :, 0] * (o_m.reshape(bsz, t, MOBA_W) @ wb_m)
              + gates[:, :, 1] * (o_d.reshape(bsz, t, DIFF_V_W) @ wb_d)
              + gates[:, :, 2] * (o_s.reshape(bsz, t, SB_W) @ wb_s))
    x = _layer_norm(DEEPNORM_ALPHA * x + merged @ w_out, ln1_g, ln1_b)

    a, b = jnp.split(x @ w_up, 2, axis=-1)
    if conv_past is None:
        conv_past = jnp.zeros((bsz, CONV_W - 1, D_FF), a.dtype)
    a_ext = jnp.concatenate([conv_past, a], 1)
    a_conv = conv_b + sum(conv_w[j] * a_ext[:, j:j + t] for j in range(CONV_W))
    h = jax.nn.gelu(a_conv, approximate=False) * b
    x = _layer_norm(DEEPNORM_ALPHA * x + h @ w_down, ln2_g, ln2_b)

    x = x + jax.nn.sigmoid(x @ w_ple_gate) * (ple @ w_ple)
    return x, k_new, v_new, a_ext[:, -(CONV_W - 1):]


def setup_inputs(seed: int = 0) -> dict:
    key = jax.random.key(seed)
    ks = jax.random.split(key, 24)
    n_pages = PAST_LEN // PAGE_SIZE
    n_used = DEC_BATCH * n_pages
    n_phys = n_used + n_used // 4

    def nrm(k, shape, s):
        return jax.random.normal(k, shape, jnp.float32) * s

    return {
        'x_prompt': nrm(ks[0], (BATCH, SEQ, D_MODEL), 1.0),
        'x_sample': nrm(ks[1], (DEC_BATCH, DEC_SEQ, D_MODEL), 1.0),
        'cache_k': nrm(ks[2], (DEPTH, n_phys, PAGE_SIZE, QK_W), 1.0),
        'cache_v': nrm(ks[3], (DEPTH, n_phys, PAGE_SIZE, V_W), 1.0),
        'state_conv': nrm(ks[4], (DEPTH, DEC_BATCH, CONV_W - 1, D_FF), 1.0),
        'page_table': jax.random.permutation(ks[5], n_phys)[:n_used].reshape(DEC_BATCH, n_pages).astype(jnp.int32),
        'p_prompt': nrm(ks[6], (DEPTH, BATCH, SEQ, PLE_DIM), 1.0),
        'p_sample': nrm(ks[7], (DEPTH, DEC_BATCH, DEC_SEQ, PLE_DIM), 1.0),
        'w_in': nrm(ks[8], (DEPTH, D_MODEL, 2 * QK_W + V_W), D_MODEL ** -0.5),
        'w_gate': nrm(ks[9], (DEPTH, D_MODEL, N_BRANCH * D_MODEL), D_MODEL ** -0.5),
        'w_branch': nrm(ks[10], (DEPTH, V_W, D_MODEL), V_W ** -0.5),
        'w_out': nrm(ks[11], (DEPTH, D_MODEL, D_MODEL), D_MODEL ** -0.5 * DEEPNORM_BETA),
        'diff_lambda': nrm(ks[12], (DEPTH, 4, DIFF_HD), 0.1),
        'diff_subln_g': 1.0 + nrm(ks[13], (DEPTH, DIFF_VD), 0.02),
        'ln1_g': 1.0 + nrm(ks[14], (DEPTH, D_MODEL), 0.02),
        'ln1_b': nrm(ks[15], (DEPTH, D_MODEL), 0.02),
        'w_up': nrm(ks[16], (DEPTH, D_MODEL, 2 * D_FF), D_MODEL ** -0.5),
        'conv_w': nrm(ks[17], (DEPTH, CONV_W, D_FF), CONV_W ** -0.5),
        'conv_b': nrm(ks[18], (DEPTH, D_FF), 0.02),
        'w_down': nrm(ks[19], (DEPTH, D_FF, D_MODEL), D_FF ** -0.5 * DEEPNORM_BETA),
        'ln2_g': 1.0 + nrm(ks[20], (DEPTH, D_MODEL), 0.02),
        'ln2_b': nrm(ks[21], (DEPTH, D_MODEL), 0.02),
        'w_ple': nrm(ks[22], (DEPTH, PLE_DIM, D_MODEL), PLE_DIM ** -0.5 * 0.5),
        'w_ple_gate': nrm(ks[23], (DEPTH, D_MODEL, D_MODEL), D_MODEL ** -0.5),
    }


def reference(x_prompt, x_sample, cache_k, cache_v, state_conv, page_table, p_prompt, p_sample,
              w_in, w_gate, w_branch, w_out, diff_lambda, diff_subln_g, ln1_g, ln1_b,
              w_up, conv_w, conv_b, w_down, ln2_g, ln2_b, w_ple, w_ple_gate):
    dec_batch, n_pages = page_table.shape
    past_len = n_pages * cache_k.shape[2]
    yp, ys = x_prompt, x_sample
    kp, vp, cp, ksm, vsm, csm = [], [], [], [], [], []
    for i in range(DEPTH):
        layer_w = (w_in[i], w_gate[i], w_branch[i], w_out[i], diff_lambda[i], diff_subln_g[i],
                   ln1_g[i], ln1_b[i], w_up[i], conv_w[i], conv_b[i], w_down[i],
                   ln2_g[i], ln2_b[i], w_ple[i], w_ple_gate[i])
        yp, k_i, v_i, c_i = _trunk_layer(i, yp, p_prompt[i], None, None, None, *layer_w)
        kp.append(k_i)
        vp.append(v_i)
        cp.append(c_i)
        k_past = cache_k[i][page_table].reshape(dec_batch, past_len, QK_W)
        v_past = cache_v[i][page_table].reshape(dec_batch, past_len, V_W)
        ys, k_j, v_j, c_j = _trunk_layer(i, ys, p_sample[i], k_past, v_past, state_conv[i], *layer_w)
        ksm.append(k_j)
        vsm.append(v_j)
        csm.append(c_j)
    return (yp, ys, jnp.stack(kp), jnp.stack(vp), jnp.stack(cp), jnp.stack(ksm), jnp.stack(vsm), jnp.stack(csm))
```

```python
import functools
import math

import numpy as np
import jax
import jax.numpy as jnp
from jax import lax
from jax.experimental import pallas as pl
from jax.experimental.pallas import tpu as pltpu

F32 = jnp.float32
BF16 = jnp.bfloat16

HEAD_DIM = 64
MOBA_HEADS = 4
MOBA_BLOCK = 256
MOBA_TOPK = 3
DIFF_HEADS = 4
SB_HEADS = 4
MOBA_W = MOBA_HEADS * HEAD_DIM
DIFF_W = DIFF_HEADS * 2 * HEAD_DIM
SB_W = SB_HEADS * HEAD_DIM
QK_W = MOBA_W + DIFF_W + SB_W
N_GROUPS = QK_W // HEAD_DIM
N_BRANCH = 3
N_ALIBI = MOBA_HEADS + DIFF_HEADS
CONV_W = 3
LN_EPS = 1e-5
QK_SCALE = HEAD_DIM ** -0.5
NEG = -1e30
LANES = 128
SUBLANES = 8
VMEM_LIMIT = 56 * 1024 * 1024

_SLOPES = [2.0 ** (-8.0 * h / N_ALIBI) for h in range(1, N_ALIBI + 1)]
MOBA_SLOPES = _SLOPES[0::2]
DIFF_SLOPES = _SLOPES[1::2]


def _nt(a, b):
    return lax.dot_general(a, b, (((1,), (1,)), ((), ())), preferred_element_type=F32)


def _nn(a, b):
    return jnp.dot(a, b, preferred_element_type=F32)


def _split(x):
    hi = x.astype(BF16)
    lo = (x - hi.astype(F32)).astype(BF16)
    return hi, lo


def _sigmoid(x):
    return 1.0 / (1.0 + jnp.exp(-x))


def _layer_norm(x, g, b):
    mu = jnp.mean(x, axis=-1, keepdims=True)
    xc = x - mu
    var = jnp.mean(xc * xc, axis=-1, keepdims=True)
    return xc * lax.rsqrt(var + LN_EPS) * g + b


def _params(*sem):
    return pltpu.CompilerParams(dimension_semantics=sem, vmem_limit_bytes=VMEM_LIMIT)


def _proj_kernel(x_ref, w_ref, k_ref, v_ref, qb_ref, kb_ref, vb_ref, qmf_ref, *rest, n_blk):
    xb = x_ref[...].astype(BF16)
    q = _nn(xb, w_ref[:, 0:QK_W])
    k = _nn(xb, w_ref[:, QK_W:2 * QK_W])
    v = _nn(xb, w_ref[:, 2 * QK_W:3 * QK_W])
    k_ref[...] = k
    v_ref[...] = v
    qb_ref[...] = (q * QK_SCALE).astype(BF16)
    kb_ref[...] = k.astype(BF16)
    vb_ref[...] = v.astype(BF16)
    qmf_ref[...] = q[:, :MOBA_W]
    if n_blk:
        kmean_ref = rest[0]
        for j in range(n_blk):
            blk = k[j * MOBA_BLOCK:(j + 1) * MOBA_BLOCK, :MOBA_W]
            kmean_ref[j:j + 1, :] = jnp.sum(blk, axis=0, keepdims=True) * (1.0 / MOBA_BLOCK)


def _proj(x2d, w, tm, with_kmean):
    m, d = x2d.shape
    n_blk = tm // MOBA_BLOCK if with_kmean else 0
    row = lambda i: (i, 0)
    out_shape = [jax.ShapeDtypeStruct((m, QK_W), F32), jax.ShapeDtypeStruct((m, QK_W), F32),
                 jax.ShapeDtypeStruct((m, QK_W), BF16), jax.ShapeDtypeStruct((m, QK_W), BF16),
                 jax.ShapeDtypeStruct((m, QK_W), BF16), jax.ShapeDtypeStruct((m, MOBA_W), F32)]
    out_specs = [pl.BlockSpec((tm, QK_W), row)] * 5 + [pl.BlockSpec((tm, MOBA_W), row)]
    if with_kmean:
        out_shape.append(jax.ShapeDtypeStruct((m // tm, n_blk, MOBA_W), F32))
        out_specs.append(pl.BlockSpec((None, n_blk, MOBA_W), lambda i: (i, 0, 0)))
    return pl.pallas_call(
        functools.partial(_proj_kernel, n_blk=n_blk),
        grid=(m // tm,),
        in_specs=[pl.BlockSpec((tm, d), row), pl.BlockSpec(w.shape, lambda i: (0, 0))],
        out_specs=out_specs, out_shape=out_shape,
        compiler_params=_params("arbitrary"), name="proj",
    )(x2d, w)


def _head_select(parts, lane_head):
    out = parts[-1]
    for h in range(len(parts) - 2, -1, -1):
        out = jnp.where(lane_head == h, parts[h], out)
    return out


def _moba_kernel(q_ref, qf_ref, kmean_ref, k_ref, v_ref, o_ref, selb_ref, *, n_blk):
    tq = MOBA_BLOCK
    qi = pl.program_id(1)
    lane_head = lax.broadcasted_iota(jnp.int32, (1, MOBA_W), 1) // HEAD_DIM
    q = q_ref[...]
    qh = [jnp.where(lane_head == h, q, jnp.zeros_like(q)) for h in range(MOBA_HEADS)]

    qf_hi, qf_lo = _split(qf_ref[...])
    km = kmean_ref[...]
    km = jnp.concatenate([km, jnp.zeros((LANES - n_blk, MOBA_W), F32)], axis=0)
    blk_lane = lax.broadcasted_iota(jnp.int32, (tq, LANES), 1)
    for h in range(MOBA_HEADS):
        kh_hi, kh_lo = _split(jnp.where(lane_head == h, km, 0.0))
        g = _nt(qf_hi, kh_hi) + _nt(qf_hi, kh_lo) + _nt(qf_lo, kh_hi)
        cnt = jnp.zeros((tq, LANES), F32)
        for mb in range(n_blk - 1):
            gm = g[:, mb:mb + 1]
            tie = jnp.where(blk_lane > mb, 1.0, 0.0)
            beats = jnp.where(gm > g, 1.0, jnp.where(gm == g, tie, 0.0))
            cnt = cnt + beats * jnp.where(mb < qi, 1.0, 0.0)
        sel = jnp.where(blk_lane < qi, jnp.where(cnt < MOBA_TOPK, 0.0, NEG), NEG)
        selb_ref[h] = sel

    col = lax.broadcasted_iota(jnp.int32, (1, tq), 1).astype(F32)
    rel = (lax.broadcasted_iota(jnp.int32, (tq, tq), 0)
           - lax.broadcasted_iota(jnp.int32, (tq, tq), 1))

    def step(n, diag, carry):
        ms, ls, acc = carry
        start = pl.multiple_of(n * tq, tq)
        kblk = k_ref[pl.ds(start, tq), :]
        vblk = v_ref[pl.ds(start, tq), :]
        offs = ((n - qi) * tq).astype(F32)
        new_ms, new_ls, alphas, pvs = [], [], [], []
        for h in range(MOBA_HEADS):
            s = _nt(qh[h], kblk) + MOBA_SLOPES[h] * (col + offs)
            if diag:
                s = jnp.where(rel >= 0, s, NEG)
            else:
                selcol = jnp.sum(jnp.where(blk_lane == n, selb_ref[h], 0.0), axis=1, keepdims=True)
                s = s + selcol
            m_new = jnp.maximum(ms[h], jnp.max(s, axis=1, keepdims=True))
            alpha = jnp.exp(ms[h] - m_new)
            p = jnp.exp(s - m_new)
            new_ls.append(alpha * ls[h] + jnp.sum(p, axis=1, keepdims=True))
            new_ms.append(m_new)
            alphas.append(alpha)
            pvs.append(_nn(p.astype(BF16), vblk))
        acc = acc * _head_select(alphas, lane_head) + _head_select(pvs, lane_head)
        return tuple(new_ms), tuple(new_ls), acc

    init = (tuple(jnp.full((tq, 1), NEG, F32) for _ in range(MOBA_HEADS)),
            tuple(jnp.zeros((tq, 1), F32) for _ in range(MOBA_HEADS)),
            jnp.zeros((tq, MOBA_W), F32))
    carry = step(qi, True, init)
    ms, ls, acc = lax.fori_loop(0, qi, lambda n, c: step(n, False, c), carry)
    inv = [1.0 / l for l in ls]
    o_ref[...] = (acc * _head_select(inv, lane_head)).astype(o_ref.dtype)


def _moba(qb, qmf, kmean, kb, vb):
    b, t, _ = qb.shape
    n_blk = t // MOBA_BLOCK
    return pl.pallas_call(
        functools.partial(_moba_kernel, n_blk=n_blk),
        grid=(b, n_blk),
        in_specs=[pl.BlockSpec((None, MOBA_BLOCK, MOBA_W), lambda i, j: (i, j, 0)),
                  pl.BlockSpec((None, MOBA_BLOCK, MOBA_W), lambda i, j: (i, j, 0)),
                  pl.BlockSpec((None, n_blk, MOBA_W), lambda i, j: (i, 0, 0)),
                  pl.BlockSpec((None, t, MOBA_W), lambda i, j: (i, 0, 0)),
                  pl.BlockSpec((None, t, MOBA_W), lambda i, j: (i, 0, 0))],
        out_specs=pl.BlockSpec((None, MOBA_BLOCK, MOBA_W), lambda i, j: (i, j, 0)),
        out_shape=jax.ShapeDtypeStruct((b, t, MOBA_W), BF16),
        scratch_shapes=[pltpu.VMEM((MOBA_HEADS, MOBA_BLOCK, LANES), F32)],
        compiler_params=_params("arbitrary", "arbitrary"), name="moba",
    )(qb, qmf, kmean, kb, vb)


def _strict_upper(n):
    r = lax.broadcasted_iota(jnp.int32, (n, n), 0)
    c = lax.broadcasted_iota(jnp.int32, (n, n), 1)
    return jnp.where(r > c, 1.0, 0.0).astype(BF16)


def _softplus(z):
    return jnp.maximum(z, 0.0) + jnp.log(1.0 + jnp.exp(-jnp.abs(z)))


def _sb_kernel(q_ref, k_ref, v_ref, o_ref, *, tq):
    qi = pl.program_id(1)
    lane_head = lax.broadcasted_iota(jnp.int32, (1, SB_W), 1) // HEAD_DIM
    q = q_ref[...]
    qh = [jnp.where(lane_head == h, q, jnp.zeros_like(q)) for h in range(SB_HEADS)]
    upper = _strict_upper(tq)
    rel = (lax.broadcasted_iota(jnp.int32, (tq, tq), 0)
           - lax.broadcasted_iota(jnp.int32, (tq, tq), 1))

    def step(n, diag, carry):
        cs, acc = carry
        start = pl.multiple_of(n * tq, tq)
        kblk = k_ref[pl.ds(start, tq), :]
        vblk = v_ref[pl.ds(start, tq), :]
        new_cs, pvs = [], []
        for h in range(SB_HEADS):
            z = _nt(qh[h], kblk)
            sp = _softplus(z)
            lk = -sp
            if diag:
                lk = jnp.where(rel > 0, lk, 0.0)
            lk_hi, lk_lo = _split(lk)
            later = _nn(lk_hi, upper) + _nn(lk_lo, upper)
            a = jnp.exp((z - sp) + later + cs[h])
            if diag:
                a = jnp.where(rel > 0, a, 0.0)
            new_cs.append(cs[h] + jnp.sum(lk, axis=1, keepdims=True))
            pvs.append(_nn(a.astype(BF16), vblk))
        return tuple(new_cs), acc + _head_select(pvs, lane_head)

    init = (tuple(jnp.zeros((tq, 1), F32) for _ in range(SB_HEADS)), jnp.zeros((tq, SB_W), F32))
    carry = step(qi, True, init)
    _, acc = lax.fori_loop(0, qi, lambda i, c: step(qi - 1 - i, False, c), carry)
    o_ref[...] = acc.astype(o_ref.dtype)


def _sb(qb, kb, vb, tq=256):
    b, t, _ = qb.shape
    cb = (QK_W - SB_W) // SB_W
    return pl.pallas_call(
        functools.partial(_sb_kernel, tq=tq),
        grid=(b, t // tq),
        in_specs=[pl.BlockSpec((None, tq, SB_W), lambda i, j: (i, j, cb)),
                  pl.BlockSpec((None, t, SB_W), lambda i, j: (i, 0, cb)),
                  pl.BlockSpec((None, t, SB_W), lambda i, j: (i, 0, cb))],
        out_specs=pl.BlockSpec((None, tq, SB_W), lambda i, j: (i, j, 0)),
        out_shape=jax.ShapeDtypeStruct((b, t, SB_W), BF16),
        compiler_params=_params("arbitrary", "arbitrary"), name="sb",
    )(qb, kb, vb)


def _diff_lambda(lam_ref, lam_init):
    dl = lam_ref[...]
    s1 = jnp.sum(dl[0:1] * dl[1:2], axis=1, keepdims=True)
    s2 = jnp.sum(dl[2:3] * dl[3:4], axis=1, keepdims=True)
    return jnp.exp(s1) - jnp.exp(s2) + lam_init


def _diff_kernel(q_ref, k_ref, v_ref, lam_ref, g_ref, o_ref, *, tq, lam_init):
    hd = 2 * HEAD_DIM
    h = pl.program_id(1)
    qi = pl.program_id(2)
    slope = jnp.float32(DIFF_SLOPES[-1])
    for i in range(DIFF_HEADS - 2, -1, -1):
        slope = jnp.where(h == i, jnp.float32(DIFF_SLOPES[i]), slope)
    lane = lax.broadcasted_iota(jnp.int32, (1, hd), 1)
    q = q_ref[...]
    qc = [jnp.where(lane < HEAD_DIM, q, jnp.zeros_like(q)),
          jnp.where(lane >= HEAD_DIM, q, jnp.zeros_like(q))]
    col = lax.broadcasted_iota(jnp.int32, (1, tq), 1).astype(F32)
    rel = (lax.broadcasted_iota(jnp.int32, (tq, tq), 0)
           - lax.broadcasted_iota(jnp.int32, (tq, tq), 1))

    def step(n, diag, carry):
        ms, ls, accs = carry
        start = pl.multiple_of(n * tq, tq)
        kblk = k_ref[pl.ds(start, tq), :]
        vblk = v_ref[pl.ds(start, tq), :]
        bias = slope * (col + ((n - qi) * tq).astype(F32))
        new = ([], [], [])
        for c in range(2):
            s = _nt(qc[c], kblk) + bias
            if diag:
                s = jnp.where(rel >= 0, s, NEG)
            m_new = jnp.maximum(ms[c], jnp.max(s, axis=1, keepdims=True))
            alpha = jnp.exp(ms[c] - m_new)
            p = jnp.exp(s - m_new)
            new[0].append(m_new)
            new[1].append(alpha * ls[c] + jnp.sum(p, axis=1, keepdims=True))
            new[2].append(alpha * accs[c] + _nn(p.astype(BF16), vblk))
        return tuple(new[0]), tuple(new[1]), tuple(new[2])

    init = (tuple(jnp.full((tq, 1), NEG, F32) for _ in range(2)),
            tuple(jnp.zeros((tq, 1), F32) for _ in range(2)),
            tuple(jnp.zeros((tq, hd), F32) for _ in range(2)))
    carry = step(qi, True, init)
    _, ls, accs = lax.fori_loop(0, qi, lambda n, c: step(n, False, c), carry)
    lam = _diff_lambda(lam_ref, lam_init)
    o = accs[0] * (1.0 / ls[0]) - lam * (accs[1] * (1.0 / ls[1]))
    ms2 = jnp.mean(o * o, axis=1, keepdims=True)
    o_ref[...] = (o * lax.rsqrt(ms2 + LN_EPS) * g_ref[...] * (1.0 - lam_init)).astype(o_ref.dtype)


def _diff(qb, kb, vb, lam_p, subln_g, lam_init, tq=256):
    b, t, _ = qb.shape
    hd = 2 * HEAD_DIM
    c0 = MOBA_W // hd
    return pl.pallas_call(
        functools.partial(_diff_kernel, tq=tq, lam_init=lam_init),
        grid=(b, DIFF_HEADS, t // tq),
        in_specs=[pl.BlockSpec((None, tq, hd), lambda i, h, j: (i, j, c0 + h)),
                  pl.BlockSpec((None, t, hd), lambda i, h, j: (i, 0, c0 + h)),
                  pl.BlockSpec((None, t, hd), lambda i, h, j: (i, 0, c0 + h)),
                  pl.BlockSpec(lam_p.shape, lambda i, h, j: (0, 0)),
                  pl.BlockSpec(subln_g.shape, lambda i, h, j: (0, 0))],
        out_specs=pl.BlockSpec((None, tq, hd), lambda i, h, j: (i, j, h)),
        out_shape=jax.ShapeDtypeStruct((b, t, DIFF_W), BF16),
        compiler_params=_params("arbitrary", "arbitrary", "arbitrary"), name="diff",
    )(qb, kb, vb, lam_p, subln_g)


def _sample_attn_kernel(pt_ref, qbd_ref, qg_ref, kn_ref, vn_ref, lam_ref, g_ref, *rest,
                        n_blk, n_new, tq, past_len, lam_init, pages_per_blk):
    del pt_ref
    kp_refs = rest[:pages_per_blk]
    vp_refs = rest[pages_per_blk:2 * pages_per_blk]
    o_ref = rest[2 * pages_per_blk]
    acc_ref, m_ref, l_ref, cs_ref, gall_ref, mall_ref, lall_ref, oall_ref = rest[2 * pages_per_blk + 1:]
    rows = tq * N_GROUPS
    j = pl.program_id(1)

    row = lax.broadcasted_iota(jnp.int32, (rows, 1), 0)
    grp = row % N_GROUPS
    qidx = row // N_GROUPS
    kind = jnp.where(grp < MOBA_HEADS, 0, jnp.where(grp < MOBA_HEADS + 2 * DIFF_HEADS, 1, 2))
    is_moba = kind == 0
    is_diff = kind == 1
    is_sb = kind == 2
    slope = jnp.zeros((rows, 1), F32)
    for h in range(MOBA_HEADS):
        slope = jnp.where(grp == h, MOBA_SLOPES[h], slope)
    for h in range(DIFF_HEADS):
        slope = jnp.where(grp - MOBA_HEADS - 2 * h == 0, DIFF_SLOPES[h], slope)
        slope = jnp.where(grp - MOBA_HEADS - 2 * h == 1, DIFF_SLOPES[h], slope)
    blk_lane = lax.broadcasted_iota(jnp.int32, (rows, LANES), 1)

    @pl.when(j == 0)
    def _():
        acc_ref[...] = jnp.zeros_like(acc_ref)
        m_ref[...] = jnp.full_like(m_ref, NEG)
        l_ref[...] = jnp.zeros_like(l_ref)
        cs_ref[...] = jnp.zeros_like(cs_ref)
        gall_ref[...] = jnp.zeros_like(gall_ref)
        mall_ref[...] = jnp.full_like(mall_ref, NEG)
        lall_ref[...] = jnp.zeros_like(lall_ref)

    def process(kb, vb, n, pos0, limit):
        nk = kb.shape[0]
        s_raw = _nt(qbd_ref[...], kb)
        col = lax.broadcasted_iota(jnp.int32, (1, nk), 1)
        s = s_raw + slope * (col.astype(F32) + pos0)
        if limit is not None:
            valid = col < limit
            s = jnp.where(valid, s, NEG)
        bmax = jnp.max(s, axis=1, keepdims=True)
        m_old = m_ref[...]
        m_use = jnp.where(is_moba, bmax, jnp.maximum(m_old, bmax))
        alpha = jnp.exp(m_old - m_use)
        p = jnp.exp(s - m_use)
        psum = jnp.sum(p, axis=1, keepdims=True)
        l_ref[...] = jnp.where(is_diff, alpha * l_ref[...] + psum, l_ref[...])
        m_ref[...] = jnp.where(is_diff, m_use, m_old)
        mall_ref[...] = jnp.where(blk_lane == n, bmax, mall_ref[...])
        lall_ref[...] = jnp.where(blk_lane == n, psum, lall_ref[...])
        sp = _softplus(s_raw)
        lk = -sp
        if limit is not None:
            lk = jnp.where(valid, lk, 0.0)
        lk_hi, lk_lo = _split(lk)
        upper = _strict_upper(nk)
        later = _nn(lk_hi, upper) + _nn(lk_lo, upper)
        cs = cs_ref[...]
        a = jnp.exp((s_raw - sp) + later + cs)
        if limit is not None:
            a = jnp.where(valid, a, 0.0)
        cs_ref[...] = cs + jnp.sum(lk, axis=1, keepdims=True)
        pmat = jnp.where(is_sb, a, p).astype(BF16)
        r = _nn(pmat, vb)
        fac = jnp.where(is_diff, alpha, jnp.where(is_sb, 1.0, 0.0))
        acc_ref[...] = acc_ref[...] * fac + r
        oall_ref[n] = r[:, :MOBA_W]

    @pl.when(j == 0)
    def _():
        limit = jnp.where(is_sb, qidx, qidx + 1)
        process(kn_ref[...], vn_ref[...], n_blk, 0.0, limit)

    @pl.when(j > 0)
    def _():
        n = n_blk - j
        kf = jnp.concatenate([r[...] for r in kp_refs], axis=0)
        vf = jnp.concatenate([r[...] for r in vp_refs], axis=0)
        kmean = jnp.sum(kf[:, :MOBA_W], axis=0, keepdims=True) * (1.0 / MOBA_BLOCK)
        gate = jnp.sum(qg_ref[...] * kmean, axis=1, keepdims=True)
        gall_ref[...] = jnp.where(blk_lane == n, gate, gall_ref[...])
        pos0 = (n * MOBA_BLOCK - past_len).astype(F32)
        process(kf.astype(BF16), vf.astype(BF16), n, pos0, None)

    @pl.when(j == n_blk)
    def _():
        g = gall_ref[...]
        cnt = jnp.zeros((rows, LANES), F32)
        for mb in range(n_blk):
            gm = g[:, mb:mb + 1]
            tie = jnp.where(blk_lane > mb, 1.0, 0.0)
            cnt = cnt + jnp.where(gm > g, 1.0, jnp.where(gm == g, tie, 0.0))
        rank_lim = jnp.where(blk_lane < n_blk, float(MOBA_TOPK), jnp.where(blk_lane == n_blk, 1e9, -1.0))
        sel = cnt < rank_lim
        mall = jnp.where(sel, mall_ref[...], NEG)
        m_tot = jnp.max(mall, axis=1, keepdims=True)
        w = jnp.where(sel, jnp.exp(mall - m_tot), 0.0)
        l_tot = jnp.sum(w * lall_ref[...], axis=1, keepdims=True)
        o_m = jnp.zeros((rows, MOBA_W), F32)
        for nb in range(n_blk + 1):
            o_m = o_m + w[:, nb:nb + 1] * oall_ref[nb]
        scale = jnp.where(is_diff, 1.0 / l_ref[...], jnp.where(is_sb, 1.0, 1.0 / l_tot))
        acc = acc_ref[...]
        acc_m = jnp.where(is_moba, o_m, acc[:, :MOBA_W])
        acc = jnp.concatenate([acc_m, acc[:, MOBA_W:]], axis=1) * scale
        lam = _diff_lambda(lam_ref, lam_init)
        g16 = lax.broadcasted_iota(jnp.int32, (N_GROUPS, 1), 0)
        cgrp = lax.broadcasted_iota(jnp.int32, (1, QK_W), 1) // HEAD_DIM
        d_lo, d_hi = MOBA_HEADS, MOBA_HEADS + 2 * DIFF_HEADS
        g_diff = jnp.where(g16 >= d_lo, jnp.where(g16 < d_hi, 1, 0), 0)
        c_diff = jnp.where(cgrp >= d_lo, jnp.where(cgrp < d_hi, 1, 0), 0)
        g_key = jnp.where(g_diff == 1, d_lo + 2 * ((g16 - d_lo) // 2), g16)
        c_key = jnp.where(c_diff == 1, d_lo + 2 * ((cgrp - d_lo) // 2), cgrp)
        second_half = g_diff * ((g16 - d_lo) % 2)
        coef = jnp.where(g_key == c_key, jnp.where(second_half == 1, -lam, 1.0), 0.0)
        hd = 2 * HEAD_DIM
        gsub = g_ref[...]
        for t in range(tq):
            o_t = jnp.sum(acc[t * N_GROUPS:(t + 1) * N_GROUPS, :] * coef, axis=0, keepdims=True)
            pieces = [o_t[:, :MOBA_W]]
            for h in range(DIFF_HEADS):
                od = o_t[:, MOBA_W + h * hd:MOBA_W + (h + 1) * hd]
                ms2 = jnp.mean(od * od, axis=1, keepdims=True)
                pieces.append(od * lax.rsqrt(ms2 + LN_EPS) * gsub * (1.0 - lam_init))
            pieces.append(o_t[:, MOBA_W + DIFF_W:])
            o_ref[t:t + 1, :] = jnp.concatenate(pieces, axis=1)


def _sample_attn(layer, qb, qmf, kb_new, vb_new, cache_k, cache_v, page_table, lam_p, subln_g, lam_init):
    b, tq, _ = qb.shape
    page = cache_k.shape[2]
    n_pages = page_table.shape[1]
    past_len = n_pages * page
    assert past_len % MOBA_BLOCK == 0 and MOBA_BLOCK % page == 0 and tq <= SUBLANES
    n_blk = past_len // MOBA_BLOCK
    assert n_blk + 1 <= LANES
    ppb = MOBA_BLOCK // page
    rows = tq * N_GROUPS
    n_new = LANES

    gmask = (np.arange(QK_W)[None, :] // HEAD_DIM == np.arange(N_GROUPS)[:, None])
    qbd = jnp.where(gmask[None, None], qb[:, :, None, :], jnp.zeros((), BF16)).reshape(b, rows, QK_W)
    qg = jnp.where(gmask[None, None, :, :MOBA_W], qmf[:, :, None, :], 0.0).reshape(b, rows, MOBA_W)
    pad = ((0, 0), (0, n_new - tq), (0, 0))
    kn = jnp.pad(kb_new, pad)
    vn = jnp.pad(vb_new, pad)

    def page_map(p):
        def index_map(i, j, pt):
            n = n_blk - jnp.maximum(j, 1)
            return (layer, pt[i, n * ppb + p], 0, 0)
        return index_map

    per_b = lambda i, j, pt: (i, 0, 0)
    whole = lambda i, j, pt: (0, 0)
    in_specs = [pl.BlockSpec((None, rows, QK_W), per_b),
                pl.BlockSpec((None, rows, MOBA_W), per_b),
                pl.BlockSpec((None, n_new, QK_W), per_b),
                pl.BlockSpec((None, n_new, QK_W), per_b),
                pl.BlockSpec(lam_p.shape, whole),
                pl.BlockSpec(subln_g.shape, whole)]
    in_specs += [pl.BlockSpec((None, None, page, QK_W), page_map(p)) for p in range(ppb)]
    in_specs += [pl.BlockSpec((None, None, page, QK_W), page_map(p)) for p in range(ppb)]
    grid_spec = pltpu.PrefetchScalarGridSpec(
        num_scalar_prefetch=1, grid=(b, n_blk + 1), in_specs=in_specs,
        out_specs=pl.BlockSpec((None, tq, QK_W), per_b),
        scratch_shapes=[pltpu.VMEM((rows, QK_W), F32), pltpu.VMEM((rows, 1), F32),
                        pltpu.VMEM((rows, 1), F32), pltpu.VMEM((rows, 1), F32),
                        pltpu.VMEM((rows, LANES), F32), pltpu.VMEM((rows, LANES), F32),
                        pltpu.VMEM((rows, LANES), F32), pltpu.VMEM((n_blk + 1, rows, MOBA_W), F32)])
    return pl.pallas_call(
        functools.partial(_sample_attn_kernel, n_blk=n_blk, n_new=n_new, tq=tq, past_len=past_len,
                          lam_init=lam_init, pages_per_blk=ppb),
        grid_spec=grid_spec,
        out_shape=jax.ShapeDtypeStruct((b, tq, QK_W), F32),
        compiler_params=_params("arbitrary", "arbitrary"), name="sample_attn",
    )(page_table, qbd, qg, kn, vn, lam_p, subln_g, *([cache_k] * ppb), *([cache_v] * ppb))


def _post_kernel(x_ref, om_ref, od_ref, os_ref, wg_ref, wb_ref, wo_ref, g_ref, b_ref, y_ref, *, alpha):
    x = x_ref[...]
    d = x.shape[1]
    gates = _sigmoid(_nn(x.astype(BF16), wg_ref[...]))
    merged = (gates[:, 0:d] * _nn(om_ref[...], wb_ref[0:MOBA_W, :])
              + gates[:, d:2 * d] * _nn(od_ref[...], wb_ref[MOBA_W:MOBA_W + DIFF_W, :])
              + gates[:, 2 * d:3 * d] * _nn(os_ref[...], wb_ref[MOBA_W + DIFF_W:, :]))
    mix = _nn(merged.astype(BF16), wo_ref[...])
    y_ref[...] = _layer_norm(alpha * x + mix, g_ref[...], b_ref[...])


def _post(x2d, o_m, o_d, o_s, w_gate, w_branch, w_out, ln_g, ln_b, alpha, tm):
    m, d = x2d.shape
    row = lambda i: (i, 0)
    full = lambda a: pl.BlockSpec(a.shape, lambda i: (0, 0))
    return pl.pallas_call(
        functools.partial(_post_kernel, alpha=alpha),
        grid=(m // tm,),
        in_specs=[pl.BlockSpec((tm, d), row), pl.BlockSpec((tm, MOBA_W), row),
                  pl.BlockSpec((tm, DIFF_W), row), pl.BlockSpec((tm, SB_W), row),
                  full(w_gate), full(w_branch), full(w_out), full(ln_g), full(ln_b)],
        out_specs=pl.BlockSpec((tm, d), row),
        out_shape=jax.ShapeDtypeStruct((m, d), F32),
        compiler_params=_params("arbitrary"), name="post",
    )(x2d, o_m, o_d, o_s, w_gate, w_branch, w_out, ln_g, ln_b)


def _gelu(x):
    return 0.5 * x * (1.0 + lax.erf(x * (2.0 ** -0.5)))


def _ffn_kernel(*refs, alpha, tm, seq, has_past):
    if has_past:
        x_ref, ple_ref, p1_ref, p2_ref = refs[:4]
        refs = refs[4:]
    else:
        x_ref, ple_ref = refs[:2]
        refs = refs[2:]
    (wup_ref, cw_ref, cb_ref, wdn_ref, g_ref, b_ref, wple_ref, wpg_ref,
     y_ref, conv_ref, aext_ref) = refs
    dff = cw_ref.shape[1]
    ti = pl.program_id(1)
    x = x_ref[...]
    up = _nn(x.astype(BF16), wup_ref[...])
    a = up[:, :dff]
    gate_in = up[:, dff:]

    @pl.when(ti == 0)
    def _():
        aext_ref[0:SUBLANES, :] = jnp.zeros((SUBLANES, dff), F32)

    aext_ref[SUBLANES:SUBLANES + tm, :] = a
    a1 = aext_ref[SUBLANES - 1:SUBLANES - 1 + tm, :]
    a2 = aext_ref[SUBLANES - 2:SUBLANES - 2 + tm, :]
    if has_past:
        t = lax.broadcasted_iota(jnp.int32, (tm, 1), 0) % seq
        a1 = jnp.where(t >= 1, a1, p1_ref[...])
        a2 = jnp.where(t >= 2, a2, p2_ref[...])
        conv_ref[...] = a
    else:
        aext_ref[0:SUBLANES, :] = a[tm - SUBLANES:, :]

        @pl.when(ti == pl.num_programs(1) - 1)
        def _():
            conv_ref[...] = a[tm - SUBLANES:, :]

    a_conv = cb_ref[...] + (cw_ref[0:1, :] * a2 + cw_ref[1:2, :] * a1 + cw_ref[2:3, :] * a)
    hid = _gelu(a_conv) * gate_in
    ffn = _nn(hid.astype(BF16), wdn_ref[...])
    y = _layer_norm(alpha * x + ffn, g_ref[...], b_ref[...])
    pg = _sigmoid(_nn(y.astype(BF16), wpg_ref[...]))
    y_ref[...] = y + pg * _nn(ple_ref[...].astype(BF16), wple_ref[...])


def _ffn(x2d, ple2d, past, w_up, conv_w, conv_b, w_down, ln_g, ln_b, w_ple, w_pg, alpha, tm, seq):
    m, d = x2d.shape
    dff = conv_w.shape[1]
    has_past = past is not None
    if has_past:
        assert m == tm and tm % seq == 0
        grid = (1, 1)
        nb = 1
    else:
        assert seq % tm == 0 and tm >= SUBLANES
        nb = seq // tm
        grid = (m // seq, nb)
    row = lambda i, j: (i * nb + j, 0)
    full = lambda a: pl.BlockSpec(a.shape, lambda i, j: (0,) * a.ndim)
    in_specs = [pl.BlockSpec((tm, d), row), pl.BlockSpec((tm, ple2d.shape[1]), row)]
    args = [x2d, ple2d]
    if has_past:
        in_specs += [pl.BlockSpec((tm, dff), row)] * 2
        args += list(past)
        conv_shape = jax.ShapeDtypeStruct((m, dff), F32)
        conv_spec = pl.BlockSpec((tm, dff), row)
    else:
        conv_shape = jax.ShapeDtypeStruct((m // seq, SUBLANES, dff), F32)
        conv_spec = pl.BlockSpec((None, SUBLANES, dff), lambda i, j: (i, 0, 0))
    weights = [w_up, conv_w, conv_b, w_down, ln_g, ln_b, w_ple, w_pg]
    in_specs += [full(w) for w in weights]
    return pl.pallas_call(
        functools.partial(_ffn_kernel, alpha=alpha, tm=tm, seq=seq, has_past=has_past),
        grid=grid, in_specs=in_specs,
        out_specs=[pl.BlockSpec((tm, d), row), conv_spec],
        out_shape=[jax.ShapeDtypeStruct((m, d), F32), conv_shape],
        scratch_shapes=[pltpu.VMEM((tm + SUBLANES, dff), F32)],
        compiler_params=_params("arbitrary", "arbitrary"), name="ffn",
    )(*args, *weights)


def kernel(x_prompt, x_sample, cache_k, cache_v, state_conv, page_table, p_prompt, p_sample, w_in, w_gate, w_branch, w_out, diff_lambda, diff_subln_g, ln1_g, ln1_b, w_up, conv_w, conv_b, w_down, ln2_g, ln2_b, w_ple, w_ple_gate):
    depth = w_in.shape[0]
    alpha = (2.0 * depth) ** 0.25
    bp, tp, d = x_prompt.shape
    bs, ts, _ = x_sample.shape
    dff = conv_w.shape[2]
    tm_p = 512 if tp % 512 == 0 else MOBA_BLOCK
    tm_f = 256
    assert tp % MOBA_BLOCK == 0
    yp = x_prompt.reshape(bp * tp, d)
    ys = x_sample.reshape(bs * ts, d)
    outs = {k: [] for k in ("kp", "vp", "cp", "ks", "vs", "cs")}
    row2 = lambda a: a.reshape(1, -1)
    for i in range(depth):
        lam_init = 0.8 - 0.6 * math.exp(-0.3 * i)
        wi, wg, wbr, wo = (w[i].astype(BF16) for w in (w_in, w_gate, w_branch, w_out))
        wu, wd, wpl, wpg = (w[i].astype(BF16) for w in (w_up, w_down, w_ple, w_ple_gate))
        lam_p, subg = diff_lambda[i], row2(diff_subln_g[i])
        l1g, l1b, l2g, l2b, cb = (row2(a[i]) for a in (ln1_g, ln1_b, ln2_g, ln2_b, conv_b))
        cw = conv_w[i]

        k, v, qb, kb, vb, qmf, kmean = _proj(yp, wi, tm_p, True)
        to3 = lambda a: a.reshape(bp, tp, a.shape[-1])
        qb3, kb3, vb3 = to3(qb), to3(kb), to3(vb)
        o_m = _moba(qb3, to3(qmf), kmean.reshape(bp, tp // MOBA_BLOCK, MOBA_W), kb3, vb3)
        o_d = _diff(qb3, kb3, vb3, lam_p, subg, lam_init)
        o_s = _sb(qb3, kb3, vb3)
        x1 = _post(yp, o_m.reshape(-1, MOBA_W), o_d.reshape(-1, DIFF_W), o_s.reshape(-1, SB_W),
                   wg, wbr, wo, l1g, l1b, alpha, tm_f)
        yp, conv = _ffn(x1, p_prompt[i].reshape(bp * tp, -1), None, wu, cw, cb, wd, l2g, l2b, wpl, wpg,
                        alpha, tm_f, tp)
        outs["kp"].append(k.reshape(bp, tp, -1))
        outs["vp"].append(v.reshape(bp, tp, -1))
        outs["cp"].append(conv[:, SUBLANES - (CONV_W - 1):, :])

        ms = bs * ts
        k, v, qb, kb, vb, qmf = _proj(ys, wi, ms, False)
        to3 = lambda a: a.reshape(bs, ts, a.shape[-1])
        o = _sample_attn(i, to3(qb), to3(qmf), to3(kb), to3(vb), cache_k, cache_v, page_table,
                         lam_p, subg, lam_init).reshape(ms, -1).astype(BF16)
        x1 = _post(ys, o[:, :MOBA_W], o[:, MOBA_W:MOBA_W + DIFF_W], o[:, MOBA_W + DIFF_W:],
                   wg, wbr, wo, l1g, l1b, alpha, ms)
        st = state_conv[i]
        zeros = jnp.zeros((bs, ts - 1, dff), F32)
        p1 = jnp.concatenate([st[:, 1:2], zeros], axis=1).reshape(ms, dff)
        p2 = jnp.concatenate([st, zeros[:, 1:]], axis=1).reshape(ms, dff)
        ys, a_full = _ffn(x1, p_sample[i].reshape(ms, -1), (p1, p2), wu, cw, cb, wd, l2g, l2b, wpl, wpg,
                          alpha, ms, ts)
        outs["ks"].append(k.reshape(bs, ts, -1))
        outs["vs"].append(v.reshape(bs, ts, -1))
        outs["cs"].append(a_full.reshape(bs, ts, dff)[:, ts - (CONV_W - 1):, :])
    st = lambda name: jnp.stack(outs[name])
    return (yp.reshape(bp, tp, d), ys.reshape(bs, ts, d), st("kp"), st("vp"), st("cp"),
            st("ks"), st("vs"), st("cs"))
```

```python
import functools
import math

import numpy as np
import jax
import jax.numpy as jnp
from jax import lax
from jax.experimental import pallas as pl
from jax.experimental.pallas import tpu as pltpu

F32 = jnp.float32
BF16 = jnp.bfloat16

HEAD_DIM = 64
MOBA_HEADS = 4
MOBA_BLOCK = 256
MOBA_TOPK = 3
DIFF_HEADS = 4
SB_HEADS = 4
MOBA_W = MOBA_HEADS * HEAD_DIM
DIFF_W = DIFF_HEADS * 2 * HEAD_DIM
SB_W = SB_HEADS * HEAD_DIM
QK_W = MOBA_W + DIFF_W + SB_W
N_GROUPS = QK_W // HEAD_DIM
N_BRANCH = 3
N_ALIBI = MOBA_HEADS + DIFF_HEADS
CONV_W = 3
LN_EPS = 1e-5
QK_SCALE = HEAD_DIM ** -0.5
NEG = -1e30
LANES = 128
SUBLANES = 8
VMEM_LIMIT = 56 * 1024 * 1024
TQ = MOBA_BLOCK
GATE_ROWS = 2 * SUBLANES
DECODE_BLOCKS_PER_STEP = 4

_SLOPES = [2.0 ** (-8.0 * h / N_ALIBI) for h in range(1, N_ALIBI + 1)]
MOBA_SLOPES = _SLOPES[0::2]
DIFF_SLOPES = _SLOPES[1::2]


def _nt(a, b):
    return lax.dot_general(a, b, (((1,), (1,)), ((), ())), preferred_element_type=F32)


def _nn(a, b):
    return jnp.dot(a, b, preferred_element_type=F32)


def _split(x):
    hi = x.astype(BF16)
    lo = (x - hi.astype(F32)).astype(BF16)
    return hi, lo


def _sigmoid(x):
    return 1.0 / (1.0 + jnp.exp(-x))


def _softplus(z):
    return jnp.maximum(z, 0.0) + jnp.log(1.0 + jnp.exp(-jnp.abs(z)))


def _layer_norm(x, g, b):
    mu = jnp.mean(x, axis=-1, keepdims=True)
    xc = x - mu
    var = jnp.mean(xc * xc, axis=-1, keepdims=True)
    return xc * lax.rsqrt(var + LN_EPS) * g + b


def _params(*sem):
    return pltpu.CompilerParams(dimension_semantics=sem, vmem_limit_bytes=VMEM_LIMIT)


def _iota(shape, dim):
    return lax.broadcasted_iota(jnp.int32, shape, dim)


def _proj_kernel(*refs, n_blk, n_carried):
    x_ref, w_ref = refs[:2]
    k_ref, v_ref, qb_ref, kb_ref, vb_ref, qmf_ref, *rest = refs[2 + n_carried:]
    xb = x_ref[...].astype(BF16)
    q = _nn(xb, w_ref[:, 0:QK_W])
    k = _nn(xb, w_ref[:, QK_W:2 * QK_W])
    v = _nn(xb, w_ref[:, 2 * QK_W:3 * QK_W])
    k_ref[...] = k
    v_ref[...] = v
    qb_ref[...] = (q * QK_SCALE).astype(BF16)
    kb_ref[...] = k.astype(BF16)
    vb_ref[...] = v.astype(BF16)
    qmf_ref[...] = q[:, :MOBA_W]
    if n_blk:
        kmean_ref = rest[0]
        for j in range(n_blk):
            blk = k[j * MOBA_BLOCK:(j + 1) * MOBA_BLOCK, :MOBA_W]
            kmean_ref[j:j + 1, :] = jnp.sum(blk, axis=0, keepdims=True) * (1.0 / MOBA_BLOCK)


def _proj(x2d, w, tm, with_kmean, layer, depth, kv_all=()):
    m, d = x2d.shape
    n_blk = tm // MOBA_BLOCK if with_kmean else 0
    row = lambda i: (i, 0)
    out_shape = [jax.ShapeDtypeStruct((depth, m, QK_W), F32), jax.ShapeDtypeStruct((depth, m, QK_W), F32),
                 jax.ShapeDtypeStruct((m, QK_W), BF16), jax.ShapeDtypeStruct((m, QK_W), BF16),
                 jax.ShapeDtypeStruct((m, QK_W), BF16), jax.ShapeDtypeStruct((m, MOBA_W), F32)]
    out_specs = ([pl.BlockSpec((None, tm, QK_W), lambda i: (layer, i, 0))] * 2
                 + [pl.BlockSpec((tm, QK_W), row)] * 3 + [pl.BlockSpec((tm, MOBA_W), row)])
    if with_kmean:
        out_shape.append(jax.ShapeDtypeStruct((m // tm, n_blk, MOBA_W), F32))
        out_specs.append(pl.BlockSpec((None, n_blk, MOBA_W), lambda i: (i, 0, 0)))
    in_specs = [pl.BlockSpec((tm, d), row), pl.BlockSpec(w.shape, lambda i: (0, 0))]
    in_specs += [pl.BlockSpec(memory_space=pl.ANY)] * len(kv_all)
    return pl.pallas_call(
        functools.partial(_proj_kernel, n_blk=n_blk, n_carried=len(kv_all)),
        grid=(m // tm,),
        in_specs=in_specs, out_specs=out_specs, out_shape=out_shape,
        input_output_aliases={2 + i: i for i in range(len(kv_all))},
        compiler_params=_params("arbitrary"), name="proj",
    )(x2d, w, *kv_all)


def _fill_vt(v_ref, vt_ref):
    t = v_ref.shape[0]
    for j in range(t // TQ):
        blk = v_ref[j * TQ:(j + 1) * TQ, :].astype(F32)
        vt_ref[:, j * TQ:(j + 1) * TQ] = blk.T.astype(BF16)


def _stack_heads_t(q_bf16, n_heads, width):
    qt = q_bf16.astype(F32).T
    row_head = _iota((n_heads * width, 1), 0) // width
    return jnp.concatenate([jnp.where(row_head == h, qt, 0.0).astype(BF16) for h in range(n_heads)], axis=1)


def _lane_slopes(slopes, n_lanes):
    lane_head = _iota((1, n_lanes), 1) // TQ
    out = jnp.zeros((1, n_lanes), F32)
    for h, s in enumerate(slopes):
        out = jnp.where(lane_head == h, s, out)
    return out


def _softmax_step(s, m, l, extra):
    cand = jnp.max(s, axis=0, keepdims=True) + extra
    m_new = jnp.maximum(m, cand)
    alpha = jnp.exp(m - m_new)
    p = jnp.exp(s - (m_new - extra))
    l_new = alpha * l + jnp.sum(p, axis=0, keepdims=True)
    return p, alpha, m_new, l_new


def _moba_kernel(q_ref, qf_ref, kmean_ref, k_ref, v_ref, o_ref, vt_ref, bias_ref, selb_ref, acc_ref, *, n_blk):
    nq = MOBA_HEADS * TQ
    qi = pl.program_id(1)

    @pl.when(qi == 0)
    def _():
        _fill_vt(v_ref, vt_ref)
        bias_ref[...] = _iota((TQ, nq), 0).astype(F32) * _lane_slopes(MOBA_SLOPES, nq)

    qt_all = _stack_heads_t(q_ref[...], MOBA_HEADS, HEAD_DIM)
    slope_row = _lane_slopes(MOBA_SLOPES, nq)

    qft_hi, qft_lo = _split(qf_ref[...].T)
    km = kmean_ref[...]
    km = jnp.concatenate([km, jnp.zeros((GATE_ROWS - n_blk, MOBA_W), F32)], axis=0)
    lane_head = _iota((1, MOBA_W), 1) // HEAD_DIM
    blk_row = _iota((GATE_ROWS, 1), 0)
    for h in range(MOBA_HEADS):
        kh_hi, kh_lo = _split(jnp.where(lane_head == h, km, 0.0))
        g = _nn(kh_hi, qft_hi) + _nn(kh_hi, qft_lo) + _nn(kh_lo, qft_hi)
        cnt = jnp.zeros((GATE_ROWS, TQ), F32)
        for mb in range(n_blk - 1):
            gm = g[mb:mb + 1, :]
            tie = jnp.where(blk_row > mb, 1.0, 0.0)
            beats = jnp.where(gm > g, 1.0, jnp.where(gm == g, tie, 0.0))
            cnt = cnt + beats * jnp.where(mb < qi, 1.0, 0.0)
        sel = jnp.where(blk_row < qi, jnp.where(cnt < MOBA_TOPK, 0.0, NEG), NEG)
        selb_ref[:, h * TQ:(h + 1) * TQ] = sel

    acc_ref[...] = jnp.zeros_like(acc_ref)

    def scores(n, diag):
        s = _nn(k_ref[pl.ds(pl.multiple_of(n * TQ, TQ), TQ), :], qt_all) + bias_ref[...]
        if diag:
            s = jnp.where(_iota((TQ, nq), 0) <= _iota((TQ, nq), 1) % TQ, s, NEG)
        return s

    def attend(s, n, diag, m, l):
        start = pl.multiple_of(n * TQ, TQ)
        extra = slope_row * ((n - qi) * TQ).astype(F32)
        if not diag:
            extra = extra + selb_ref[pl.ds(n, 1), :]
        p, alpha, m, l = _softmax_step(s, m, l, extra)
        pb = p.astype(BF16)
        for h in range(MOBA_HEADS):
            rows = slice(h * HEAD_DIM, (h + 1) * HEAD_DIM)
            cols = slice(h * TQ, (h + 1) * TQ)
            pv = _nn(vt_ref[rows, pl.ds(start, TQ)], pb[:, cols])
            acc_ref[rows, :] = alpha[:, cols] * acc_ref[rows, :] + pv
        return m, l

    carry = attend(scores(qi, True), qi, True, jnp.full((1, nq), NEG, F32), jnp.zeros((1, nq), F32))
    _, l = lax.fori_loop(0, qi, lambda n, c: attend(scores(n, False), n, False, *c), carry)
    inv = 1.0 / l
    ot = jnp.concatenate([acc_ref[h * HEAD_DIM:(h + 1) * HEAD_DIM, :] * inv[:, h * TQ:(h + 1) * TQ]
                          for h in range(MOBA_HEADS)], axis=0)
    o_ref[...] = ot.T.astype(o_ref.dtype)


def _moba(qb, qmf, kmean, kb, vb):
    b, t, _ = qb.shape
    n_blk = t // TQ
    assert n_blk <= GATE_ROWS
    nq = MOBA_HEADS * TQ
    return pl.pallas_call(
        functools.partial(_moba_kernel, n_blk=n_blk),
        grid=(b, n_blk),
        in_specs=[pl.BlockSpec((None, TQ, MOBA_W), lambda i, j: (i, j, 0)),
                  pl.BlockSpec((None, TQ, MOBA_W), lambda i, j: (i, j, 0)),
                  pl.BlockSpec((None, n_blk, MOBA_W), lambda i, j: (i, 0, 0)),
                  pl.BlockSpec((None, t, MOBA_W), lambda i, j: (i, 0, 0)),
                  pl.BlockSpec((None, t, MOBA_W), lambda i, j: (i, 0, 0))],
        out_specs=pl.BlockSpec((None, TQ, MOBA_W), lambda i, j: (i, j, 0)),
        out_shape=jax.ShapeDtypeStruct((b, t, MOBA_W), BF16),
        scratch_shapes=[pltpu.VMEM((MOBA_W, t), BF16), pltpu.VMEM((TQ, nq), F32),
                        pltpu.VMEM((GATE_ROWS, nq), F32), pltpu.VMEM((MOBA_W, TQ), F32)],
        compiler_params=_params("arbitrary", "arbitrary"), name="moba",
    )(qb, qmf, kmean, kb, vb)


def _suffix_matrix(n, transposed):
    r = _iota((n, n), 0)
    c = _iota((n, n), 1)
    return jnp.where((c > r) if transposed else (r > c), 1.0, 0.0).astype(BF16)


def _sb_kernel(q_ref, k_ref, v_ref, o_ref, vt_ref, acc_ref):
    nq = SB_HEADS * TQ
    qi = pl.program_id(1)

    @pl.when(qi == 0)
    def _():
        _fill_vt(v_ref, vt_ref)

    qt_all = _stack_heads_t(q_ref[...], SB_HEADS, HEAD_DIM)
    later_mat = _suffix_matrix(TQ, True)
    acc_ref[...] = jnp.zeros_like(acc_ref)

    def scores(n):
        return _nn(k_ref[pl.ds(pl.multiple_of(n * TQ, TQ), TQ), :], qt_all)

    def attend(z, n, diag, cs):
        start = pl.multiple_of(n * TQ, TQ)
        sp = _softplus(z)
        lk = -sp
        if diag:
            before = _iota((TQ, nq), 0) < _iota((TQ, nq), 1) % TQ
            lk = jnp.where(before, lk, 0.0)
        lk_hi, lk_lo = _split(lk)
        later = _nn(later_mat, lk_hi) + _nn(later_mat, lk_lo)
        a = jnp.exp((z - sp) + later + cs)
        if diag:
            a = jnp.where(before, a, 0.0)
        ab = a.astype(BF16)
        for h in range(SB_HEADS):
            rows = slice(h * HEAD_DIM, (h + 1) * HEAD_DIM)
            acc_ref[rows, :] += _nn(vt_ref[rows, pl.ds(start, TQ)], ab[:, h * TQ:(h + 1) * TQ])
        return cs + jnp.sum(lk, axis=0, keepdims=True)

    cs = attend(scores(qi), qi, True, jnp.zeros((1, nq), F32))
    lax.fori_loop(0, qi, lambda i, c: attend(scores(qi - 1 - i), qi - 1 - i, False, c), cs)
    o_ref[...] = acc_ref[...].T.astype(o_ref.dtype)


def _sb(qb, kb, vb):
    b, t, _ = qb.shape
    cb = (QK_W - SB_W) // SB_W
    return pl.pallas_call(
        _sb_kernel,
        grid=(b, t // TQ),
        in_specs=[pl.BlockSpec((None, TQ, SB_W), lambda i, j: (i, j, cb)),
                  pl.BlockSpec((None, t, SB_W), lambda i, j: (i, 0, cb)),
                  pl.BlockSpec((None, t, SB_W), lambda i, j: (i, 0, cb))],
        out_specs=pl.BlockSpec((None, TQ, SB_W), lambda i, j: (i, j, 0)),
        out_shape=jax.ShapeDtypeStruct((b, t, SB_W), BF16),
        scratch_shapes=[pltpu.VMEM((SB_W, t), BF16), pltpu.VMEM((SB_W, TQ), F32)],
        compiler_params=_params("arbitrary", "arbitrary"), name="sb",
    )(qb, kb, vb)


def _diff_lambda(lam_ref, lam_init):
    dl = lam_ref[...]
    s1 = jnp.sum(dl[0:1] * dl[1:2], axis=1, keepdims=True)
    s2 = jnp.sum(dl[2:3] * dl[3:4], axis=1, keepdims=True)
    return jnp.exp(s1) - jnp.exp(s2) + lam_init


def _diff_kernel(q_ref, k_ref, v_ref, lam_ref, g_ref, o_ref, vt_ref, bias_ref, acc_ref, *, lam_init):
    hd = 2 * HEAD_DIM
    nq = 2 * TQ
    h = pl.program_id(1)
    qi = pl.program_id(2)
    slope = jnp.float32(DIFF_SLOPES[-1])
    for i in range(DIFF_HEADS - 2, -1, -1):
        slope = jnp.where(h == i, jnp.float32(DIFF_SLOPES[i]), slope)

    @pl.when(qi == 0)
    def _():
        _fill_vt(v_ref, vt_ref)
        bias_ref[...] = _iota((TQ, nq), 0).astype(F32) * slope

    qt_all = _stack_heads_t(q_ref[...], 2, HEAD_DIM)
    acc_ref[...] = jnp.zeros_like(acc_ref)

    def scores(n, diag):
        s = _nn(k_ref[pl.ds(pl.multiple_of(n * TQ, TQ), TQ), :], qt_all) + bias_ref[...]
        if diag:
            s = jnp.where(_iota((TQ, nq), 0) <= _iota((TQ, nq), 1) % TQ, s, NEG)
        return s

    def attend(s, n, m, l):
        start = pl.multiple_of(n * TQ, TQ)
        extra = jnp.full((1, nq), slope * ((n - qi) * TQ).astype(F32), F32)
        p, alpha, m, l = _softmax_step(s, m, l, extra)
        acc_ref[...] = alpha * acc_ref[...] + _nn(vt_ref[:, pl.ds(start, TQ)], p.astype(BF16))
        return m, l

    s_first = scores(0, False)
    m, l = attend(scores(qi, True), qi, jnp.full((1, nq), NEG, F32), jnp.zeros((1, nq), F32))

    def body(n, carry):
        m, l, s = carry
        s_next = scores(jnp.minimum(n + 1, qi), False)
        m, l = attend(s, n, m, l)
        return m, l, s_next

    _, l, _ = lax.fori_loop(0, qi, body, (m, l, s_first))
    acc = acc_ref[...] * (1.0 / l)
    lam = _diff_lambda(lam_ref, lam_init)
    o = acc[:, :TQ] - lam * acc[:, TQ:]
    ms2 = jnp.mean(o * o, axis=0, keepdims=True)
    on = o * lax.rsqrt(ms2 + LN_EPS) * g_ref[...] * (1.0 - lam_init)
    o_ref[...] = on.T.astype(o_ref.dtype)


def _diff(qb, kb, vb, lam_p, subln_g_col, lam_init):
    b, t, _ = qb.shape
    hd = 2 * HEAD_DIM
    c0 = MOBA_W // hd
    return pl.pallas_call(
        functools.partial(_diff_kernel, lam_init=lam_init),
        grid=(b, DIFF_HEADS, t // TQ),
        in_specs=[pl.BlockSpec((None, TQ, hd), lambda i, h, j: (i, j, c0 + h)),
                  pl.BlockSpec((None, t, hd), lambda i, h, j: (i, 0, c0 + h)),
                  pl.BlockSpec((None, t, hd), lambda i, h, j: (i, 0, c0 + h)),
                  pl.BlockSpec(lam_p.shape, lambda i, h, j: (0, 0)),
                  pl.BlockSpec(subln_g_col.shape, lambda i, h, j: (0, 0))],
        out_specs=pl.BlockSpec((None, TQ, hd), lambda i, h, j: (i, j, h)),
        out_shape=jax.ShapeDtypeStruct((b, t, DIFF_W), BF16),
        scratch_shapes=[pltpu.VMEM((hd, t), BF16), pltpu.VMEM((TQ, 2 * TQ), F32),
                        pltpu.VMEM((hd, 2 * TQ), F32)],
        compiler_params=_params("arbitrary", "arbitrary", "arbitrary"), name="diff",
    )(qb, kb, vb, lam_p, subln_g_col)


def _sample_attn_kernel(pt_ref, qbd_ref, qg_ref, kn_ref, vn_ref, lam_ref, g_ref, *rest,
                        n_blk, tq, past_len, lam_init, n_pages_step, blocks_per_step):
    del pt_ref
    kp_refs = rest[:n_pages_step]
    vp_refs = rest[n_pages_step:2 * n_pages_step]
    o_ref = rest[2 * n_pages_step]
    acc_ref, m_ref, l_ref, cs_ref, gall_ref, mall_ref, lall_ref, oall_ref = rest[2 * n_pages_step + 1:]
    rows = tq * N_GROUPS
    rest_w = QK_W - MOBA_W
    j = pl.program_id(1)
    last = pl.num_programs(1) - 1

    row = _iota((rows, 1), 0)
    grp = row % N_GROUPS
    qidx = row // N_GROUPS
    kind = jnp.where(grp < MOBA_HEADS, 0, jnp.where(grp < MOBA_HEADS + 2 * DIFF_HEADS, 1, 2))
    is_moba = kind == 0
    is_diff = kind == 1
    is_sb = kind == 2
    slope = jnp.zeros((rows, 1), F32)
    for h in range(MOBA_HEADS):
        slope = jnp.where(grp == h, MOBA_SLOPES[h], slope)
    for h in range(DIFF_HEADS):
        slope = jnp.where(grp - MOBA_HEADS - 2 * h == 0, DIFF_SLOPES[h], slope)
        slope = jnp.where(grp - MOBA_HEADS - 2 * h == 1, DIFF_SLOPES[h], slope)
    blk_lane = _iota((rows, LANES), 1)

    @pl.when(j == 0)
    def _():
        acc_ref[...] = jnp.zeros_like(acc_ref)
        m_ref[...] = jnp.full_like(m_ref, NEG)
        l_ref[...] = jnp.zeros_like(l_ref)
        cs_ref[...] = jnp.zeros_like(cs_ref)
        gall_ref[...] = jnp.zeros_like(gall_ref)
        mall_ref[...] = jnp.full_like(mall_ref, NEG)
        lall_ref[...] = jnp.zeros_like(lall_ref)

    def process(kb, vb, seg, n_first, pos0, limit):
        nk = kb.shape[0]
        n_seg = nk // seg
        s_raw = _nt(qbd_ref[...], kb)
        col = _iota((1, nk), 1)
        s = s_raw + slope * (col.astype(F32) + pos0)
        if limit is not None:
            valid = col < limit
            s = jnp.where(valid, s, NEG)
        segs = [slice(i * seg, (i + 1) * seg) for i in range(n_seg)]
        seg_max = [jnp.max(s[:, sl], axis=1, keepdims=True) for sl in segs]
        m_old = m_ref[...]
        m_new = m_old
        for sm in seg_max:
            m_new = jnp.maximum(m_new, sm)
        alpha = jnp.exp(m_old - m_new)
        sp = _softplus(s_raw)
        lk = -sp
        if limit is not None:
            lk = jnp.where(valid, lk, 0.0)
        later_mat = _suffix_matrix(seg, False)
        run = cs_ref[...]
        a_segs = [None] * n_seg
        for i in reversed(range(n_seg)):
            lk_hi, lk_lo = _split(lk[:, segs[i]])
            later = _nn(lk_hi, later_mat) + _nn(lk_lo, later_mat) + run
            a_segs[i] = jnp.exp((s_raw[:, segs[i]] - sp[:, segs[i]]) + later)
            run = run + jnp.sum(lk[:, segs[i]], axis=1, keepdims=True)
        cs_ref[...] = run
        psum = jnp.zeros((rows, 1), F32)
        p_segs = []
        mall, lall = mall_ref[...], lall_ref[...]
        for i in range(n_seg):
            p = jnp.exp(s[:, segs[i]] - jnp.where(is_moba, seg_max[i], m_new))
            ps = jnp.sum(p, axis=1, keepdims=True)
            psum = psum + ps
            mall = jnp.where(blk_lane == n_first + i, seg_max[i], mall)
            lall = jnp.where(blk_lane == n_first + i, ps, lall)
            a = a_segs[i]
            if limit is not None:
                a = jnp.where(valid[:, segs[i]], a, 0.0)
            p_segs.append(jnp.where(is_sb, a, p).astype(BF16))
        mall_ref[...] = mall
        lall_ref[...] = lall
        l_ref[...] = jnp.where(is_diff, alpha * l_ref[...] + psum, l_ref[...])
        m_ref[...] = jnp.where(is_diff, m_new, m_old)
        for i in range(n_seg):
            oall_ref[n_first + i] = _nn(p_segs[i], vb[segs[i], :MOBA_W])
        pmat = p_segs[0] if n_seg == 1 else jnp.concatenate(p_segs, axis=1)
        fac = jnp.where(is_diff, alpha, jnp.where(is_sb, 1.0, 0.0))
        acc_ref[...] = acc_ref[...] * fac + _nn(pmat, vb[:, MOBA_W:])

    @pl.when(j == 0)
    def _():
        limit = jnp.where(is_sb, qidx, qidx + 1)
        process(kn_ref[...], vn_ref[...], kn_ref.shape[0], n_blk, 0.0, limit)

    @pl.when(j > 0)
    def _():
        n_first = n_blk - j * blocks_per_step
        page = kp_refs[0].shape[0]
        ppb = MOBA_BLOCK // page
        gall = gall_ref[...]
        for i in range(blocks_per_step):
            ksum = jnp.zeros((1, MOBA_W), F32)
            for p in range(ppb):
                ksum = ksum + jnp.sum(kp_refs[i * ppb + p][:, :MOBA_W], axis=0, keepdims=True)
            gate = jnp.sum(qg_ref[...] * (ksum * (1.0 / MOBA_BLOCK)), axis=1, keepdims=True)
            gall = jnp.where(blk_lane == n_first + i, gate, gall)
        gall_ref[...] = gall
        kb = jnp.concatenate([r[...].astype(BF16) for r in kp_refs], axis=0)
        vb = jnp.concatenate([r[...].astype(BF16) for r in vp_refs], axis=0)
        pos0 = (n_first * MOBA_BLOCK - past_len).astype(F32)
        process(kb, vb, MOBA_BLOCK, n_first, pos0, None)

    @pl.when(j == last)
    def _():
        g = gall_ref[...]
        cnt = jnp.zeros((rows, LANES), F32)
        for mb in range(n_blk):
            gm = g[:, mb:mb + 1]
            tie = jnp.where(blk_lane > mb, 1.0, 0.0)
            cnt = cnt + jnp.where(gm > g, 1.0, jnp.where(gm == g, tie, 0.0))
        rank_lim = jnp.where(blk_lane < n_blk, float(MOBA_TOPK), jnp.where(blk_lane == n_blk, 1e9, -1.0))
        sel = cnt < rank_lim
        mall = jnp.where(sel, mall_ref[...], NEG)
        m_tot = jnp.max(mall, axis=1, keepdims=True)
        w = jnp.where(sel, jnp.exp(mall - m_tot), 0.0)
        l_tot = jnp.sum(w * lall_ref[...], axis=1, keepdims=True)
        o_m = jnp.zeros((rows, MOBA_W), F32)
        for nb in range(n_blk + 1):
            o_m = o_m + w[:, nb:nb + 1] * oall_ref[nb]
        o_m = o_m * jnp.where(is_moba, 1.0 / l_tot, 0.0)
        rest = acc_ref[...] * jnp.where(is_diff, 1.0 / l_ref[...], jnp.where(is_sb, 1.0, 0.0))
        acc = jnp.concatenate([o_m, rest], axis=1)
        lam = _diff_lambda(lam_ref, lam_init)
        g16 = _iota((N_GROUPS, 1), 0)
        cgrp = _iota((1, QK_W), 1) // HEAD_DIM
        d_lo, d_hi = MOBA_HEADS, MOBA_HEADS + 2 * DIFF_HEADS
        g_diff = jnp.where(g16 >= d_lo, jnp.where(g16 < d_hi, 1, 0), 0)
        c_diff = jnp.where(cgrp >= d_lo, jnp.where(cgrp < d_hi, 1, 0), 0)
        g_key = jnp.where(g_diff == 1, d_lo + 2 * ((g16 - d_lo) // 2), g16)
        c_key = jnp.where(c_diff == 1, d_lo + 2 * ((cgrp - d_lo) // 2), cgrp)
        second_half = g_diff * ((g16 - d_lo) % 2)
        coef = jnp.where(g_key == c_key, jnp.where(second_half == 1, -lam, 1.0), 0.0)
        hd = 2 * HEAD_DIM
        gsub = g_ref[...]
        for t in range(tq):
            o_t = jnp.sum(acc[t * N_GROUPS:(t + 1) * N_GROUPS, :] * coef, axis=0, keepdims=True)
            pieces = [o_t[:, :MOBA_W]]
            for h in range(DIFF_HEADS):
                od = o_t[:, MOBA_W + h * hd:MOBA_W + (h + 1) * hd]
                ms2 = jnp.mean(od * od, axis=1, keepdims=True)
                pieces.append(od * lax.rsqrt(ms2 + LN_EPS) * gsub * (1.0 - lam_init))
            pieces.append(o_t[:, MOBA_W + DIFF_W:])
            o_ref[t:t + 1, :] = jnp.concatenate(pieces, axis=1)


def _sample_attn(layer, qb, qmf, kb_new, vb_new, cache_k, cache_v, page_table, lam_p, subln_g, lam_init):
    b, tq, _ = qb.shape
    page = cache_k.shape[2]
    n_pages = page_table.shape[1]
    past_len = n_pages * page
    assert past_len % MOBA_BLOCK == 0 and MOBA_BLOCK % page == 0 and tq <= SUBLANES
    n_blk = past_len // MOBA_BLOCK
    assert n_blk + 1 <= LANES
    bps = math.gcd(n_blk, DECODE_BLOCKS_PER_STEP)
    pps = bps * (MOBA_BLOCK // page)
    rows = tq * N_GROUPS
    n_new = LANES

    gmask = (np.arange(QK_W)[None, :] // HEAD_DIM == np.arange(N_GROUPS)[:, None])
    qbd = jnp.where(gmask[None, None], qb[:, :, None, :], jnp.zeros((), BF16)).reshape(b, rows, QK_W)
    qg = jnp.where(gmask[None, None, :, :MOBA_W], qmf[:, :, None, :], 0.0).reshape(b, rows, MOBA_W)
    pad = ((0, 0), (0, n_new - tq), (0, 0))
    kn = jnp.pad(kb_new, pad)
    vn = jnp.pad(vb_new, pad)

    def page_map(p):
        def index_map(i, j, pt):
            n_first = n_blk - jnp.maximum(j, 1) * bps
            return (layer, pt[i, n_first * (MOBA_BLOCK // page) + p], 0, 0)
        return index_map

    per_b = lambda i, j, pt: (i, 0, 0)
    whole = lambda i, j, pt: (0, 0)
    in_specs = [pl.BlockSpec((None, rows, QK_W), per_b),
                pl.BlockSpec((None, rows, MOBA_W), per_b),
                pl.BlockSpec((None, n_new, QK_W), per_b),
                pl.BlockSpec((None, n_new, QK_W), per_b),
                pl.BlockSpec(lam_p.shape, whole),
                pl.BlockSpec(subln_g.shape, whole)]
    in_specs += [pl.BlockSpec((None, None, page, QK_W), page_map(p)) for p in range(pps)]
    in_specs += [pl.BlockSpec((None, None, page, QK_W), page_map(p)) for p in range(pps)]
    grid_spec = pltpu.PrefetchScalarGridSpec(
        num_scalar_prefetch=1, grid=(b, n_blk // bps + 1), in_specs=in_specs,
        out_specs=pl.BlockSpec((None, tq, QK_W), per_b),
        scratch_shapes=[pltpu.VMEM((rows, QK_W - MOBA_W), F32), pltpu.VMEM((rows, 1), F32),
                        pltpu.VMEM((rows, 1), F32), pltpu.VMEM((rows, 1), F32),
                        pltpu.VMEM((rows, LANES), F32), pltpu.VMEM((rows, LANES), F32),
                        pltpu.VMEM((rows, LANES), F32), pltpu.VMEM((n_blk + 1, rows, MOBA_W), F32)])
    return pl.pallas_call(
        functools.partial(_sample_attn_kernel, n_blk=n_blk, tq=tq, past_len=past_len,
                          lam_init=lam_init, n_pages_step=pps, blocks_per_step=bps),
        grid_spec=grid_spec,
        out_shape=jax.ShapeDtypeStruct((b, tq, QK_W), F32),
        compiler_params=_params("arbitrary", "arbitrary"), name="sample_attn",
    )(page_table, qbd, qg, kn, vn, lam_p, subln_g, *([cache_k] * pps), *([cache_v] * pps))


def _post_kernel(x_ref, om_ref, od_ref, os_ref, wg_ref, wb_ref, wo_ref, g_ref, b_ref, y_ref, *, alpha):
    x = x_ref[...]
    d = x.shape[1]
    gates = _sigmoid(_nn(x.astype(BF16), wg_ref[...]))
    merged = (gates[:, 0:d] * _nn(om_ref[...], wb_ref[0:MOBA_W, :])
              + gates[:, d:2 * d] * _nn(od_ref[...], wb_ref[MOBA_W:MOBA_W + DIFF_W, :])
              + gates[:, 2 * d:3 * d] * _nn(os_ref[...], wb_ref[MOBA_W + DIFF_W:, :]))
    mix = _nn(merged.astype(BF16), wo_ref[...])
    y_ref[...] = _layer_norm(alpha * x + mix, g_ref[...], b_ref[...])


def _post(x2d, o_m, o_d, o_s, w_gate, w_branch, w_out, ln_g, ln_b, alpha, tm):
    m, d = x2d.shape
    row = lambda i: (i, 0)
    full = lambda a: pl.BlockSpec(a.shape, lambda i: (0, 0))
    return pl.pallas_call(
        functools.partial(_post_kernel, alpha=alpha),
        grid=(m // tm,),
        in_specs=[pl.BlockSpec((tm, d), row), pl.BlockSpec((tm, MOBA_W), row),
                  pl.BlockSpec((tm, DIFF_W), row), pl.BlockSpec((tm, SB_W), row),
                  full(w_gate), full(w_branch), full(w_out), full(ln_g), full(ln_b)],
        out_specs=pl.BlockSpec((tm, d), row),
        out_shape=jax.ShapeDtypeStruct((m, d), F32),
        compiler_params=_params("arbitrary"), name="post",
    )(x2d, o_m, o_d, o_s, w_gate, w_branch, w_out, ln_g, ln_b)


def _gelu(x):
    return 0.5 * x * (1.0 + lax.erf(x * (2.0 ** -0.5)))


def _ffn_kernel(*refs, alpha, tm, seq, has_past):
    if has_past:
        x_ref, ple_ref, p1_ref, p2_ref = refs[:4]
        refs = refs[4:]
    else:
        x_ref, ple_ref = refs[:2]
        refs = refs[2:]
    (wup_ref, cw_ref, cb_ref, wdn_ref, g_ref, b_ref, wple_ref, wpg_ref,
     y_ref, conv_ref, aext_ref) = refs
    dff = cw_ref.shape[1]
    ti = pl.program_id(1)
    x = x_ref[...]
    up = _nn(x.astype(BF16), wup_ref[...])
    a = up[:, :dff]
    gate_in = up[:, dff:]

    @pl.when(ti == 0)
    def _():
        aext_ref[0:SUBLANES, :] = jnp.zeros((SUBLANES, dff), F32)

    aext_ref[SUBLANES:SUBLANES + tm, :] = a
    a1 = aext_ref[SUBLANES - 1:SUBLANES - 1 + tm, :]
    a2 = aext_ref[SUBLANES - 2:SUBLANES - 2 + tm, :]
    if has_past:
        t = _iota((tm, 1), 0) % seq
        a1 = jnp.where(t >= 1, a1, p1_ref[...])
        a2 = jnp.where(t >= 2, a2, p2_ref[...])
        conv_ref[...] = a
    else:
        aext_ref[0:SUBLANES, :] = a[tm - SUBLANES:, :]

        @pl.when(ti == pl.num_programs(1) - 1)
        def _():
            conv_ref[...] = a[tm - SUBLANES:, :]

    a_conv = cb_ref[...] + (cw_ref[0:1, :] * a2 + cw_ref[1:2, :] * a1 + cw_ref[2:3, :] * a)
    hid = _gelu(a_conv) * gate_in
    ffn = _nn(hid.astype(BF16), wdn_ref[...])
    y = _layer_norm(alpha * x + ffn, g_ref[...], b_ref[...])
    pg = _sigmoid(_nn(y.astype(BF16), wpg_ref[...]))
    y_ref[...] = y + pg * _nn(ple_ref[...].astype(BF16), wple_ref[...])


def _ffn(x2d, ple2d, past, w_up, conv_w, conv_b, w_down, ln_g, ln_b, w_ple, w_pg, alpha, tm, seq):
    m, d = x2d.shape
    dff = conv_w.shape[1]
    has_past = past is not None
    if has_past:
        assert m == tm and tm % seq == 0
        grid = (1, 1)
        nb = 1
    else:
        assert seq % tm == 0 and tm >= SUBLANES
        nb = seq // tm
        grid = (m // seq, nb)
    row = lambda i, j: (i * nb + j, 0)
    full = lambda a: pl.BlockSpec(a.shape, lambda i, j: (0,) * a.ndim)
    in_specs = [pl.BlockSpec((tm, d), row), pl.BlockSpec((tm, ple2d.shape[1]), row)]
    args = [x2d, ple2d]
    if has_past:
        in_specs += [pl.BlockSpec((tm, dff), row)] * 2
        args += list(past)
        conv_shape = jax.ShapeDtypeStruct((m, dff), F32)
        conv_spec = pl.BlockSpec((tm, dff), row)
    else:
        conv_shape = jax.ShapeDtypeStruct((m // seq, SUBLANES, dff), F32)
        conv_spec = pl.BlockSpec((None, SUBLANES, dff), lambda i, j: (i, 0, 0))
    weights = [w_up, conv_w, conv_b, w_down, ln_g, ln_b, w_ple, w_pg]
    in_specs += [full(w) for w in weights]
    return pl.pallas_call(
        functools.partial(_ffn_kernel, alpha=alpha, tm=tm, seq=seq, has_past=has_past),
        grid=grid, in_specs=in_specs,
        out_specs=[pl.BlockSpec((tm, d), row), conv_spec],
        out_shape=[jax.ShapeDtypeStruct((m, d), F32), conv_shape],
        scratch_shapes=[pltpu.VMEM((tm + SUBLANES, dff), F32)],
        compiler_params=_params("arbitrary", "arbitrary"), name="ffn",
    )(*args, *weights)


def kernel(x_prompt, x_sample, cache_k, cache_v, state_conv, page_table, p_prompt, p_sample, w_in, w_gate, w_branch, w_out, diff_lambda, diff_subln_g, ln1_g, ln1_b, w_up, conv_w, conv_b, w_down, ln2_g, ln2_b, w_ple, w_ple_gate):
    depth = w_in.shape[0]
    alpha = (2.0 * depth) ** 0.25
    bp, tp, d = x_prompt.shape
    bs, ts, _ = x_sample.shape
    dff = conv_w.shape[2]
    tm_p = 512 if tp % 512 == 0 else MOBA_BLOCK
    tm_f = 256
    assert tp % MOBA_BLOCK == 0
    yp = x_prompt.reshape(bp * tp, d)
    ys = x_sample.reshape(bs * ts, d)
    outs = {k: [] for k in ("cp", "cs")}
    kv_p, kv_s = (), ()
    row2 = lambda a: a.reshape(1, -1)
    for i in range(depth):
        lam_init = 0.8 - 0.6 * math.exp(-0.3 * i)
        wi, wg, wbr, wo = (w[i].astype(BF16) for w in (w_in, w_gate, w_branch, w_out))
        wu, wd, wpl, wpg = (w[i].astype(BF16) for w in (w_up, w_down, w_ple, w_ple_gate))
        lam_p, subg = diff_lambda[i], row2(diff_subln_g[i])
        l1g, l1b, l2g, l2b, cb = (row2(a[i]) for a in (ln1_g, ln1_b, ln2_g, ln2_b, conv_b))
        cw = conv_w[i]

        *kv_p, qb, kb, vb, qmf, kmean = _proj(yp, wi, tm_p, True, i, depth, tuple(kv_p))
        to3 = lambda a: a.reshape(bp, tp, a.shape[-1])
        qb3, kb3, vb3 = to3(qb), to3(kb), to3(vb)
        o_m = _moba(qb3, to3(qmf), kmean.reshape(bp, tp // MOBA_BLOCK, MOBA_W), kb3, vb3)
        o_d = _diff(qb3, kb3, vb3, lam_p, subg.reshape(-1, 1), lam_init)
        o_s = _sb(qb3, kb3, vb3)
        x1 = _post(yp, o_m.reshape(-1, MOBA_W), o_d.reshape(-1, DIFF_W), o_s.reshape(-1, SB_W),
                   wg, wbr, wo, l1g, l1b, alpha, tm_f)
        yp, conv = _ffn(x1, p_prompt[i].reshape(bp * tp, -1), None, wu, cw, cb, wd, l2g, l2b, wpl, wpg,
                        alpha, tm_f, tp)
        outs["cp"].append(conv[:, SUBLANES - (CONV_W - 1):, :])

        ms = bs * ts
        *kv_s, qb, kb, vb, qmf = _proj(ys, wi, ms, False, i, depth, tuple(kv_s))
        to3 = lambda a: a.reshape(bs, ts, a.shape[-1])
        o = _sample_attn(i, to3(qb), to3(qmf), to3(kb), to3(vb), cache_k, cache_v, page_table,
                         lam_p, subg, lam_init).reshape(ms, -1).astype(BF16)
        x1 = _post(ys, o[:, :MOBA_W], o[:, MOBA_W:MOBA_W + DIFF_W], o[:, MOBA_W + DIFF_W:],
                   wg, wbr, wo, l1g, l1b, alpha, ms)
        st = state_conv[i]
        zeros = jnp.zeros((bs, ts - 1, dff), F32)
        p1 = jnp.concatenate([st[:, 1:2], zeros], axis=1).reshape(ms, dff)
        p2 = jnp.concatenate([st, zeros[:, 1:]], axis=1).reshape(ms, dff)
        ys, a_full = _ffn(x1, p_sample[i].reshape(ms, -1), (p1, p2), wu, cw, cb, wd, l2g, l2b, wpl, wpg,
                          alpha, ms, ts)
        outs["cs"].append(a_full.reshape(bs, ts, dff)[:, ts - (CONV_W - 1):, :])
    st = lambda name: jnp.stack(outs[name])
    k_p, v_p = (a.reshape(depth, bp, tp, -1) for a in kv_p)
    k_s, v_s = (a.reshape(depth, bs, ts, -1) for a in kv_s)
    return (yp.reshape(bp, tp, d), ys.reshape(bs, ts, d), k_p, v_p, st("cp"), k_s, v_s, st("cs"))
```

```python
import functools
import math

import numpy as np
import jax
import jax.numpy as jnp
from jax import lax
from jax.experimental import pallas as pl
from jax.experimental.pallas import tpu as pltpu

F32 = jnp.float32
BF16 = jnp.bfloat16

HEAD_DIM = 64
MOBA_HEADS = 4
MOBA_BLOCK = 256
MOBA_TOPK = 3
DIFF_HEADS = 4
SB_HEADS = 4
MOBA_W = MOBA_HEADS * HEAD_DIM
DIFF_W = DIFF_HEADS * 2 * HEAD_DIM
SB_W = SB_HEADS * HEAD_DIM
QK_W = MOBA_W + DIFF_W + SB_W
N_GROUPS = QK_W // HEAD_DIM
N_BRANCH = 3
N_ALIBI = MOBA_HEADS + DIFF_HEADS
CONV_W = 3
LN_EPS = 1e-5
QK_SCALE = HEAD_DIM ** -0.5
NEG = -1e30
LANES = 128
SUBLANES = 8
VMEM_LIMIT = 56 * 1024 * 1024
TQ = MOBA_BLOCK
GATE_ROWS = 2 * SUBLANES
DECODE_BLOCKS_PER_STEP = 4
DIFF_HEADS_PER_CHAIN = 2
ONES_ROWS = 2 * SUBLANES
LOG2E = math.log2(math.e)

_SLOPES = [2.0 ** (-8.0 * h / N_ALIBI) for h in range(1, N_ALIBI + 1)]
MOBA_SLOPES = _SLOPES[0::2]
DIFF_SLOPES = _SLOPES[1::2]


def _nt(a, b):
    return lax.dot_general(a, b, (((1,), (1,)), ((), ())), preferred_element_type=F32)


def _nn(a, b):
    return jnp.dot(a, b, preferred_element_type=F32)


def _split(x):
    hi = x.astype(BF16)
    lo = (x - hi.astype(F32)).astype(BF16)
    return hi, lo


def _sigmoid(x):
    return 1.0 / (1.0 + jnp.exp(-x))


def _softplus(z):
    return jnp.maximum(z, 0.0) + jnp.log(1.0 + jnp.exp(-jnp.abs(z)))


def _layer_norm(x, g, b):
    mu = jnp.mean(x, axis=-1, keepdims=True)
    xc = x - mu
    var = jnp.mean(xc * xc, axis=-1, keepdims=True)
    return xc * lax.rsqrt(var + LN_EPS) * g + b


def _params(*sem):
    return pltpu.CompilerParams(dimension_semantics=sem, vmem_limit_bytes=VMEM_LIMIT)


def _iota(shape, dim):
    return lax.broadcasted_iota(jnp.int32, shape, dim)


def _proj_kernel(*refs, n_blk, n_carried):
    x_ref, w_ref = refs[:2]
    k_ref, v_ref, qb_ref, kb_ref, vb_ref, qmf_ref, *rest = refs[2 + n_carried:]
    xb = x_ref[...].astype(BF16)
    q = _nn(xb, w_ref[:, 0:QK_W])
    k = _nn(xb, w_ref[:, QK_W:2 * QK_W])
    v = _nn(xb, w_ref[:, 2 * QK_W:3 * QK_W])
    k_ref[...] = k
    v_ref[...] = v
    qb_ref[...] = (q * QK_SCALE).astype(BF16)
    kb_ref[...] = k.astype(BF16)
    vb_ref[...] = v.astype(BF16)
    qmf_ref[...] = q[:, :MOBA_W]
    if n_blk:
        kmean_ref = rest[0]
        for j in range(n_blk):
            blk = k[j * MOBA_BLOCK:(j + 1) * MOBA_BLOCK, :MOBA_W]
            kmean_ref[j:j + 1, :] = jnp.sum(blk, axis=0, keepdims=True) * (1.0 / MOBA_BLOCK)


def _proj(x2d, w, tm, with_kmean, layer, depth, kv_all=()):
    m, d = x2d.shape
    n_blk = tm // MOBA_BLOCK if with_kmean else 0
    row = lambda i: (i, 0)
    out_shape = [jax.ShapeDtypeStruct((depth, m, QK_W), F32), jax.ShapeDtypeStruct((depth, m, QK_W), F32),
                 jax.ShapeDtypeStruct((m, QK_W), BF16), jax.ShapeDtypeStruct((m, QK_W), BF16),
                 jax.ShapeDtypeStruct((m, QK_W), BF16), jax.ShapeDtypeStruct((m, MOBA_W), F32)]
    out_specs = ([pl.BlockSpec((None, tm, QK_W), lambda i: (layer, i, 0))] * 2
                 + [pl.BlockSpec((tm, QK_W), row)] * 3 + [pl.BlockSpec((tm, MOBA_W), row)])
    if with_kmean:
        out_shape.append(jax.ShapeDtypeStruct((m // tm, n_blk, MOBA_W), F32))
        out_specs.append(pl.BlockSpec((None, n_blk, MOBA_W), lambda i: (i, 0, 0)))
    in_specs = [pl.BlockSpec((tm, d), row), pl.BlockSpec(w.shape, lambda i: (0, 0))]
    in_specs += [pl.BlockSpec(memory_space=pl.ANY)] * len(kv_all)
    return pl.pallas_call(
        functools.partial(_proj_kernel, n_blk=n_blk, n_carried=len(kv_all)),
        grid=(m // tm,),
        in_specs=in_specs, out_specs=out_specs, out_shape=out_shape,
        input_output_aliases={2 + i: i for i in range(len(kv_all))},
        compiler_params=_params("arbitrary"), name="proj",
    )(x2d, w, *kv_all)


def _fill_vt(v_ref, vt_ref, n_heads, width):
    t = v_ref.shape[0]
    extra = vt_ref.shape[1] - width
    for h in range(n_heads):
        if extra:
            vt_ref[h, width:, :] = jnp.ones((extra, t), BF16)
    for j in range(t // TQ):
        blk = v_ref[j * TQ:(j + 1) * TQ, :].astype(F32).T.astype(BF16)
        for h in range(n_heads):
            vt_ref[h, :width, j * TQ:(j + 1) * TQ] = blk[h * width:(h + 1) * width, :]


def _stack_heads_t(q_bf16, n_heads, width):
    qt = q_bf16.astype(F32).T * LOG2E
    row_head = _iota((n_heads * width, 1), 0) // width
    return jnp.concatenate([jnp.where(row_head == h, qt, 0.0).astype(BF16) for h in range(n_heads)], axis=1)


def _lane_slopes(slopes, n_lanes):
    lane_head = _iota((1, n_lanes), 1) // TQ
    out = jnp.zeros((1, n_lanes), F32)
    for h, s in enumerate(slopes):
        out = jnp.where(lane_head == h, s, out)
    return out


def _softmax_step(s, m, extra):
    cand = jnp.max(s, axis=0, keepdims=True) + extra
    m_new = jnp.maximum(m, cand)
    alpha = jnp.exp2(m - m_new)
    p = jnp.exp2(s - (m_new - extra))
    return p.astype(BF16), alpha, m_new


def _moba_kernel(q_ref, qf_ref, kmean_ref, k_ref, v_ref, o_ref, vt_ref, bias_ref, selb_ref, acc_ref, *, n_blk):
    nq = MOBA_HEADS * TQ
    qi = pl.program_id(1)

    slope_row = _lane_slopes(MOBA_SLOPES, nq) * LOG2E

    @pl.when(qi == 0)
    def _():
        _fill_vt(v_ref, vt_ref, MOBA_HEADS, HEAD_DIM)
        bias_ref[...] = _iota((TQ, nq), 0).astype(F32) * slope_row

    qt_all = _stack_heads_t(q_ref[...], MOBA_HEADS, HEAD_DIM)

    qft_hi, qft_lo = _split(qf_ref[...].T)
    km = kmean_ref[...]
    km = jnp.concatenate([km, jnp.zeros((GATE_ROWS - n_blk, MOBA_W), F32)], axis=0)
    lane_head = _iota((1, MOBA_W), 1) // HEAD_DIM
    blk_row = _iota((GATE_ROWS, 1), 0)
    for h in range(MOBA_HEADS):
        kh_hi, kh_lo = _split(jnp.where(lane_head == h, km, 0.0))
        g = _nn(kh_hi, qft_hi) + _nn(kh_hi, qft_lo) + _nn(kh_lo, qft_hi)
        cnt = jnp.zeros((GATE_ROWS, TQ), F32)
        for mb in range(n_blk - 1):
            gm = g[mb:mb + 1, :]
            tie = jnp.where(blk_row > mb, 1.0, 0.0)
            beats = jnp.where(gm > g, 1.0, jnp.where(gm == g, tie, 0.0))
            cnt = cnt + beats * jnp.where(mb < qi, 1.0, 0.0)
        sel = jnp.where(blk_row < qi, jnp.where(cnt < MOBA_TOPK, 0.0, NEG), NEG)
        selb_ref[:, h * TQ:(h + 1) * TQ] = sel

    acc_ref[...] = jnp.zeros_like(acc_ref)

    def scores(n, diag):
        s = _nn(k_ref[pl.ds(pl.multiple_of(n * TQ, TQ), TQ), :], qt_all) + bias_ref[...]
        if diag:
            s = jnp.where(_iota((TQ, nq), 0) <= _iota((TQ, nq), 1) % TQ, s, NEG)
        return s

    def attend(s, n, diag, m):
        start = pl.multiple_of(n * TQ, TQ)
        extra = slope_row * ((n - qi) * TQ).astype(F32)
        if not diag:
            extra = extra + selb_ref[pl.ds(n, 1), :]
        pb, alpha, m = _softmax_step(s, m, extra)
        for h in range(MOBA_HEADS):
            cols = slice(h * TQ, (h + 1) * TQ)
            pv = _nn(vt_ref[h, :, pl.ds(start, TQ)], pb[:, cols])
            acc_ref[h] = alpha[:, cols] * acc_ref[h] + pv
        return m

    m = attend(scores(qi, True), qi, True, jnp.full((1, nq), NEG, F32))
    lax.fori_loop(0, qi, lambda n, m: attend(scores(n, False), n, False, m), m)
    outs = []
    for h in range(MOBA_HEADS):
        acc = acc_ref[h]
        outs.append(acc[:HEAD_DIM] * (1.0 / acc[HEAD_DIM:HEAD_DIM + 1]))
    o_ref[...] = jnp.concatenate(outs, axis=0).T.astype(o_ref.dtype)


def _moba(qb, qmf, kmean, kb, vb):
    b, t, _ = qb.shape
    n_blk = t // TQ
    assert n_blk <= GATE_ROWS
    nq = MOBA_HEADS * TQ
    return pl.pallas_call(
        functools.partial(_moba_kernel, n_blk=n_blk),
        grid=(b, n_blk),
        in_specs=[pl.BlockSpec((None, TQ, MOBA_W), lambda i, j: (i, j, 0)),
                  pl.BlockSpec((None, TQ, MOBA_W), lambda i, j: (i, j, 0)),
                  pl.BlockSpec((None, n_blk, MOBA_W), lambda i, j: (i, 0, 0)),
                  pl.BlockSpec((None, t, MOBA_W), lambda i, j: (i, 0, 0)),
                  pl.BlockSpec((None, t, MOBA_W), lambda i, j: (i, 0, 0))],
        out_specs=pl.BlockSpec((None, TQ, MOBA_W), lambda i, j: (i, j, 0)),
        out_shape=jax.ShapeDtypeStruct((b, t, MOBA_W), BF16),
        scratch_shapes=[pltpu.VMEM((MOBA_HEADS, HEAD_DIM + ONES_ROWS, t), BF16), pltpu.VMEM((TQ, nq), F32),
                        pltpu.VMEM((GATE_ROWS, nq), F32),
                        pltpu.VMEM((MOBA_HEADS, HEAD_DIM + ONES_ROWS, TQ), F32)],
        compiler_params=_params("arbitrary", "arbitrary"), name="moba",
    )(qb, qmf, kmean, kb, vb)


def _suffix_matrix(n, transposed):
    r = _iota((n, n), 0)
    c = _iota((n, n), 1)
    return jnp.where((c > r) if transposed else (r > c), 1.0, 0.0).astype(BF16)


def _sb_kernel(q_ref, k_ref, v_ref, o_ref, vt_ref, acc_ref):
    nq = SB_HEADS * TQ
    qi = pl.program_id(1)

    @pl.when(qi == 0)
    def _():
        _fill_vt(v_ref, vt_ref, SB_HEADS, HEAD_DIM)

    qt_all = _stack_heads_t(q_ref[...], SB_HEADS, HEAD_DIM)
    r = _iota((TQ + ONES_ROWS, 2 * TQ), 0)
    c = _iota((TQ + ONES_ROWS, 2 * TQ), 1) % TQ
    later_mat = jnp.where(r >= TQ, -1.0, jnp.where(c > r, -1.0, 0.0)).astype(BF16)
    acc_ref[...] = jnp.zeros_like(acc_ref)

    def scores(n):
        return _nn(k_ref[pl.ds(pl.multiple_of(n * TQ, TQ), TQ), :], qt_all)

    def attend(z, n, diag, cs):
        start = pl.multiple_of(n * TQ, TQ)
        sp = jnp.maximum(z, 0.0) + jnp.log(1.0 + jnp.exp2(-jnp.abs(z))) * LOG2E
        if diag:
            before = _iota((TQ, nq), 0) < _iota((TQ, nq), 1) % TQ
            sp_keep = jnp.where(before, sp, 0.0)
        else:
            sp_keep = sp
        sums = _nn(later_mat, jnp.concatenate(_split(sp_keep), axis=0))
        a = jnp.exp2((z - sp) + sums[:TQ] + cs)
        if diag:
            a = jnp.where(before, a, 0.0)
        ab = a.astype(BF16)
        for h in range(SB_HEADS):
            acc_ref[h] += _nn(vt_ref[h, :, pl.ds(start, TQ)], ab[:, h * TQ:(h + 1) * TQ])
        return cs + sums[TQ:TQ + 1]

    cs = attend(scores(qi), qi, True, jnp.zeros((1, nq), F32))
    lax.fori_loop(0, qi, lambda i, c: attend(scores(qi - 1 - i), qi - 1 - i, False, c), cs)
    o_ref[...] = jnp.concatenate([acc_ref[h] for h in range(SB_HEADS)], axis=0).T.astype(o_ref.dtype)


def _sb(qb, kb, vb):
    b, t, _ = qb.shape
    cb = (QK_W - SB_W) // SB_W
    return pl.pallas_call(
        _sb_kernel,
        grid=(b, t // TQ),
        in_specs=[pl.BlockSpec((None, TQ, SB_W), lambda i, j: (i, j, cb)),
                  pl.BlockSpec((None, t, SB_W), lambda i, j: (i, 0, cb)),
                  pl.BlockSpec((None, t, SB_W), lambda i, j: (i, 0, cb))],
        out_specs=pl.BlockSpec((None, TQ, SB_W), lambda i, j: (i, j, 0)),
        out_shape=jax.ShapeDtypeStruct((b, t, SB_W), BF16),
        scratch_shapes=[pltpu.VMEM((SB_HEADS, HEAD_DIM, t), BF16), pltpu.VMEM((SB_HEADS, HEAD_DIM, TQ), F32)],
        compiler_params=_params("arbitrary", "arbitrary"), name="sb",
    )(qb, kb, vb)


def _diff_lambda(lam_ref, lam_init):
    dl = lam_ref[...]
    s1 = jnp.sum(dl[0:1] * dl[1:2], axis=1, keepdims=True)
    s2 = jnp.sum(dl[2:3] * dl[3:4], axis=1, keepdims=True)
    return jnp.exp(s1) - jnp.exp(s2) + lam_init


def _diff_kernel(*refs, lam_init):
    hd = 2 * HEAD_DIM
    nh = DIFF_HEADS_PER_CHAIN
    nc = DIFF_HEADS // nh
    nq = nh * 2 * TQ
    q_refs, k_refs, v_refs = refs[:nc], refs[nc:2 * nc], refs[2 * nc:3 * nc]
    lam_ref, g_ref, o_ref, vt_ref, bias_ref, acc_ref = refs[3 * nc:]
    qi = pl.program_id(1)
    slope_rows = []
    for c in range(nc):
        lane_head = _iota((1, nq), 1) // (2 * TQ) + c * nh
        row = jnp.zeros((1, nq), F32)
        for i in range(DIFF_HEADS):
            row = jnp.where(lane_head == i, DIFF_SLOPES[i] * LOG2E, row)
        slope_rows.append(row)

    @pl.when(qi == 0)
    def _():
        for c in range(nc):
            _fill_vt(v_refs[c], vt_ref.at[c], nh, hd)
            bias_ref[c] = _iota((TQ, nq), 0).astype(F32) * slope_rows[c]

    qt_all = [_stack_heads_t(q_refs[c][...], 2 * nh, HEAD_DIM) for c in range(nc)]
    acc_ref[...] = jnp.zeros_like(acc_ref)

    def scores(c, n, diag):
        s = _nn(k_refs[c][pl.ds(pl.multiple_of(n * TQ, TQ), TQ), :], qt_all[c]) + bias_ref[c]
        if diag:
            s = jnp.where(_iota((TQ, nq), 0) <= _iota((TQ, nq), 1) % TQ, s, NEG)
        return s

    def attend(c, s, n, m):
        start = pl.multiple_of(n * TQ, TQ)
        extra = slope_rows[c] * ((n - qi) * TQ).astype(F32)
        pb, alpha, m = _softmax_step(s, m, extra)
        for h in range(nh):
            cols = slice(h * 2 * TQ, (h + 1) * 2 * TQ)
            pv = _nn(vt_ref[c, h, :, pl.ds(start, TQ)], pb[:, cols])
            acc_ref[c, h] = alpha[:, cols] * acc_ref[c, h] + pv
        return m

    def step(n, diag, ms):
        return tuple(attend(c, scores(c, n, diag), n, ms[c]) for c in range(nc))

    ms = step(qi, True, tuple(jnp.full((1, nq), NEG, F32) for _ in range(nc)))
    lax.fori_loop(0, qi, lambda n, ms: step(n, False, ms), ms)
    lam = _diff_lambda(lam_ref, lam_init)
    outs = []
    for c in range(nc):
        for h in range(nh):
            acc = acc_ref[c, h]
            acc = acc[:hd] * (1.0 / acc[hd:hd + 1])
            o = acc[:, :TQ] - lam * acc[:, TQ:]
            ms2 = jnp.mean(o * o, axis=0, keepdims=True)
            outs.append(o * lax.rsqrt(ms2 + LN_EPS) * g_ref[...] * (1.0 - lam_init))
    o_ref[...] = jnp.concatenate(outs, axis=0).T.astype(o_ref.dtype)


def _diff(qb, kb, vb, lam_p, subln_g_col, lam_init):
    b, t, _ = qb.shape
    nh = DIFF_HEADS_PER_CHAIN
    nc = DIFF_HEADS // nh
    w = nh * 2 * HEAD_DIM
    assert MOBA_W % w == 0 and DIFF_HEADS % nh == 0
    c0 = MOBA_W // w
    tile = lambda c: pl.BlockSpec((None, TQ, w), lambda i, j: (i, j, c0 + c))
    seq = lambda c: pl.BlockSpec((None, t, w), lambda i, j: (i, 0, c0 + c))
    rows = 2 * HEAD_DIM + ONES_ROWS
    return pl.pallas_call(
        functools.partial(_diff_kernel, lam_init=lam_init),
        grid=(b, t // TQ),
        in_specs=([tile(c) for c in range(nc)] + [seq(c) for c in range(nc)] * 2
                  + [pl.BlockSpec(lam_p.shape, lambda i, j: (0, 0)),
                     pl.BlockSpec(subln_g_col.shape, lambda i, j: (0, 0))]),
        out_specs=pl.BlockSpec((None, TQ, DIFF_W), lambda i, j: (i, j, 0)),
        out_shape=jax.ShapeDtypeStruct((b, t, DIFF_W), BF16),
        scratch_shapes=[pltpu.VMEM((nc, nh, rows, t), BF16),
                        pltpu.VMEM((nc, TQ, nh * 2 * TQ), F32),
                        pltpu.VMEM((nc, nh, rows, 2 * TQ), F32)],
        compiler_params=_params("arbitrary", "arbitrary"), name="diff",
    )(*([qb] * nc), *([kb] * nc), *([vb] * nc), lam_p, subln_g_col)


def _sample_attn_kernel(pt_ref, qbd_ref, qg_ref, kn_ref, vn_ref, lam_ref, g_ref, *rest,
                        n_blk, tq, past_len, lam_init, n_pages_step, blocks_per_step):
    del pt_ref
    kp_refs = rest[:n_pages_step]
    vp_refs = rest[n_pages_step:2 * n_pages_step]
    o_ref = rest[2 * n_pages_step]
    acc_ref, m_ref, l_ref, cs_ref, gall_ref, mall_ref, lall_ref, oall_ref = rest[2 * n_pages_step + 1:]
    rows = tq * N_GROUPS
    rest_w = QK_W - MOBA_W
    j = pl.program_id(1)
    last = pl.num_programs(1) - 1

    row = _iota((rows, 1), 0)
    grp = row % N_GROUPS
    qidx = row // N_GROUPS
    kind = jnp.where(grp < MOBA_HEADS, 0, jnp.where(grp < MOBA_HEADS + 2 * DIFF_HEADS, 1, 2))
    is_moba = kind == 0
    is_diff = kind == 1
    is_sb = kind == 2
    slope = jnp.zeros((rows, 1), F32)
    for h in range(MOBA_HEADS):
        slope = jnp.where(grp == h, MOBA_SLOPES[h], slope)
    for h in range(DIFF_HEADS):
        slope = jnp.where(grp - MOBA_HEADS - 2 * h == 0, DIFF_SLOPES[h], slope)
        slope = jnp.where(grp - MOBA_HEADS - 2 * h == 1, DIFF_SLOPES[h], slope)
    blk_lane = _iota((rows, LANES), 1)

    @pl.when(j == 0)
    def _():
        acc_ref[...] = jnp.zeros_like(acc_ref)
        m_ref[...] = jnp.full_like(m_ref, NEG)
        l_ref[...] = jnp.zeros_like(l_ref)
        cs_ref[...] = jnp.zeros_like(cs_ref)
        gall_ref[...] = jnp.zeros_like(gall_ref)
        mall_ref[...] = jnp.full_like(mall_ref, NEG)
        lall_ref[...] = jnp.zeros_like(lall_ref)

    def process(kb, vb, seg, n_first, pos0, limit):
        nk = kb.shape[0]
        n_seg = nk // seg
        s_raw = _nt(qbd_ref[...], kb)
        col = _iota((1, nk), 1)
        s = s_raw + slope * (col.astype(F32) + pos0)
        if limit is not None:
            valid = col < limit
            s = jnp.where(valid, s, NEG)
        segs = [slice(i * seg, (i + 1) * seg) for i in range(n_seg)]
        seg_max = [jnp.max(s[:, sl], axis=1, keepdims=True) for sl in segs]
        m_old = m_ref[...]
        m_new = m_old
        for sm in seg_max:
            m_new = jnp.maximum(m_new, sm)
        alpha = jnp.exp(m_old - m_new)
        sp = _softplus(s_raw)
        lk = -sp
        if limit is not None:
            lk = jnp.where(valid, lk, 0.0)
        later_mat = _suffix_matrix(seg, False)
        run = cs_ref[...]
        a_segs = [None] * n_seg
        for i in reversed(range(n_seg)):
            lk_hi, lk_lo = _split(lk[:, segs[i]])
            later = _nn(lk_hi, later_mat) + _nn(lk_lo, later_mat) + run
            a_segs[i] = jnp.exp((s_raw[:, segs[i]] - sp[:, segs[i]]) + later)
            run = run + jnp.sum(lk[:, segs[i]], axis=1, keepdims=True)
        cs_ref[...] = run
        psum = jnp.zeros((rows, 1), F32)
        p_segs = []
        mall, lall = mall_ref[...], lall_ref[...]
        for i in range(n_seg):
            p = jnp.exp(s[:, segs[i]] - jnp.where(is_moba, seg_max[i], m_new))
            ps = jnp.sum(p, axis=1, keepdims=True)
            psum = psum + ps
            mall = jnp.where(blk_lane == n_first + i, seg_max[i], mall)
            lall = jnp.where(blk_lane == n_first + i, ps, lall)
            a = a_segs[i]
            if limit is not None:
                a = jnp.where(valid[:, segs[i]], a, 0.0)
            p_segs.append(jnp.where(is_sb, a, p).astype(BF16))
        mall_ref[...] = mall
        lall_ref[...] = lall
        l_ref[...] = jnp.where(is_diff, alpha * l_ref[...] + psum, l_ref[...])
        m_ref[...] = jnp.where(is_diff, m_new, m_old)
        for i in range(n_seg):
            oall_ref[n_first + i] = _nn(p_segs[i], vb[segs[i], :MOBA_W])
        pmat = p_segs[0] if n_seg == 1 else jnp.concatenate(p_segs, axis=1)
        fac = jnp.where(is_diff, alpha, jnp.where(is_sb, 1.0, 0.0))
        acc_ref[...] = acc_ref[...] * fac + _nn(pmat, vb[:, MOBA_W:])

    @pl.when(j == 0)
    def _():
        limit = jnp.where(is_sb, qidx, qidx + 1)
        process(kn_ref[...], vn_ref[...], kn_ref.shape[0], n_blk, 0.0, limit)

    @pl.when(j > 0)
    def _():
        n_first = n_blk - j * blocks_per_step
        page = kp_refs[0].shape[0]
        ppb = MOBA_BLOCK // page
        gall = gall_ref[...]
        for i in range(blocks_per_step):
            ksum = jnp.zeros((1, MOBA_W), F32)
            for p in range(ppb):
                ksum = ksum + jnp.sum(kp_refs[i * ppb + p][:, :MOBA_W], axis=0, keepdims=True)
            gate = jnp.sum(qg_ref[...] * (ksum * (1.0 / MOBA_BLOCK)), axis=1, keepdims=True)
            gall = jnp.where(blk_lane == n_first + i, gate, gall)
        gall_ref[...] = gall
        kb = jnp.concatenate([r[...].astype(BF16) for r in kp_refs], axis=0)
        vb = jnp.concatenate([r[...].astype(BF16) for r in vp_refs], axis=0)
        pos0 = (n_first * MOBA_BLOCK - past_len).astype(F32)
        process(kb, vb, MOBA_BLOCK, n_first, pos0, None)

    @pl.when(j == last)
    def _():
        g = gall_ref[...]
        cnt = jnp.zeros((rows, LANES), F32)
        for mb in range(n_blk):
            gm = g[:, mb:mb + 1]
            tie = jnp.where(blk_lane > mb, 1.0, 0.0)
            cnt = cnt + jnp.where(gm > g, 1.0, jnp.where(gm == g, tie, 0.0))
        rank_lim = jnp.where(blk_lane < n_blk, float(MOBA_TOPK), jnp.where(blk_lane == n_blk, 1e9, -1.0))
        sel = cnt < rank_lim
        mall = jnp.where(sel, mall_ref[...], NEG)
        m_tot = jnp.max(mall, axis=1, keepdims=True)
        w = jnp.where(sel, jnp.exp(mall - m_tot), 0.0)
        l_tot = jnp.sum(w * lall_ref[...], axis=1, keepdims=True)
        o_m = jnp.zeros((rows, MOBA_W), F32)
        for nb in range(n_blk + 1):
            o_m = o_m + w[:, nb:nb + 1] * oall_ref[nb]
        o_m = o_m * jnp.where(is_moba, 1.0 / l_tot, 0.0)
        rest = acc_ref[...] * jnp.where(is_diff, 1.0 / l_ref[...], jnp.where(is_sb, 1.0, 0.0))
        acc = jnp.concatenate([o_m, rest], axis=1)
        lam = _diff_lambda(lam_ref, lam_init)
        g16 = _iota((N_GROUPS, 1), 0)
        cgrp = _iota((1, QK_W), 1) // HEAD_DIM
        d_lo, d_hi = MOBA_HEADS, MOBA_HEADS + 2 * DIFF_HEADS
        g_diff = jnp.where(g16 >= d_lo, jnp.where(g16 < d_hi, 1, 0), 0)
        c_diff = jnp.where(cgrp >= d_lo, jnp.where(cgrp < d_hi, 1, 0), 0)
        g_key = jnp.where(g_diff == 1, d_lo + 2 * ((g16 - d_lo) // 2), g16)
        c_key = jnp.where(c_diff == 1, d_lo + 2 * ((cgrp - d_lo) // 2), cgrp)
        second_half = g_diff * ((g16 - d_lo) % 2)
        coef = jnp.where(g_key == c_key, jnp.where(second_half == 1, -lam, 1.0), 0.0)
        hd = 2 * HEAD_DIM
        gsub = g_ref[...]
        for t in range(tq):
            o_t = jnp.sum(acc[t * N_GROUPS:(t + 1) * N_GROUPS, :] * coef, axis=0, keepdims=True)
            pieces = [o_t[:, :MOBA_W]]
            for h in range(DIFF_HEADS):
                od = o_t[:, MOBA_W + h * hd:MOBA_W + (h + 1) * hd]
                ms2 = jnp.mean(od * od, axis=1, keepdims=True)
                pieces.append(od * lax.rsqrt(ms2 + LN_EPS) * gsub * (1.0 - lam_init))
            pieces.append(o_t[:, MOBA_W + DIFF_W:])
            o_ref[t:t + 1, :] = jnp.concatenate(pieces, axis=1)


def _sample_attn(layer, qb, qmf, kb_new, vb_new, cache_k, cache_v, page_table, lam_p, subln_g, lam_init):
    b, tq, _ = qb.shape
    page = cache_k.shape[2]
    n_pages = page_table.shape[1]
    past_len = n_pages * page
    assert past_len % MOBA_BLOCK == 0 and MOBA_BLOCK % page == 0 and tq <= SUBLANES
    n_blk = past_len // MOBA_BLOCK
    assert n_blk + 1 <= LANES
    bps = math.gcd(n_blk, DECODE_BLOCKS_PER_STEP)
    pps = bps * (MOBA_BLOCK // page)
    rows = tq * N_GROUPS
    n_new = LANES

    gmask = (np.arange(QK_W)[None, :] // HEAD_DIM == np.arange(N_GROUPS)[:, None])
    qbd = jnp.where(gmask[None, None], qb[:, :, None, :], jnp.zeros((), BF16)).reshape(b, rows, QK_W)
    qg = jnp.where(gmask[None, None, :, :MOBA_W], qmf[:, :, None, :], 0.0).reshape(b, rows, MOBA_W)
    pad = ((0, 0), (0, n_new - tq), (0, 0))
    kn = jnp.pad(kb_new, pad)
    vn = jnp.pad(vb_new, pad)

    def page_map(p):
        def index_map(i, j, pt):
            n_first = n_blk - jnp.maximum(j, 1) * bps
            return (layer, pt[i, n_first * (MOBA_BLOCK // page) + p], 0, 0)
        return index_map

    per_b = lambda i, j, pt: (i, 0, 0)
    whole = lambda i, j, pt: (0, 0)
    in_specs = [pl.BlockSpec((None, rows, QK_W), per_b),
                pl.BlockSpec((None, rows, MOBA_W), per_b),
                pl.BlockSpec((None, n_new, QK_W), per_b),
                pl.BlockSpec((None, n_new, QK_W), per_b),
                pl.BlockSpec(lam_p.shape, whole),
                pl.BlockSpec(subln_g.shape, whole)]
    in_specs += [pl.BlockSpec((None, None, page, QK_W), page_map(p)) for p in range(pps)]
    in_specs += [pl.BlockSpec((None, None, page, QK_W), page_map(p)) for p in range(pps)]
    grid_spec = pltpu.PrefetchScalarGridSpec(
        num_scalar_prefetch=1, grid=(b, n_blk // bps + 1), in_specs=in_specs,
        out_specs=pl.BlockSpec((None, tq, QK_W), per_b),
        scratch_shapes=[pltpu.VMEM((rows, QK_W - MOBA_W), F32), pltpu.VMEM((rows, 1), F32),
                        pltpu.VMEM((rows, 1), F32), pltpu.VMEM((rows, 1), F32),
                        pltpu.VMEM((rows, LANES), F32), pltpu.VMEM((rows, LANES), F32),
                        pltpu.VMEM((rows, LANES), F32), pltpu.VMEM((n_blk + 1, rows, MOBA_W), F32)])
    return pl.pallas_call(
        functools.partial(_sample_attn_kernel, n_blk=n_blk, tq=tq, past_len=past_len,
                          lam_init=lam_init, n_pages_step=pps, blocks_per_step=bps),
        grid_spec=grid_spec,
        out_shape=jax.ShapeDtypeStruct((b, tq, QK_W), F32),
        compiler_params=_params("arbitrary", "arbitrary"), name="sample_attn",
    )(page_table, qbd, qg, kn, vn, lam_p, subln_g, *([cache_k] * pps), *([cache_v] * pps))


def _post_kernel(x_ref, om_ref, od_ref, os_ref, wg_ref, wb_ref, wo_ref, g_ref, b_ref, y_ref, *, alpha):
    x = x_ref[...]
    d = x.shape[1]
    gates = _sigmoid(_nn(x.astype(BF16), wg_ref[...]))
    merged = (gates[:, 0:d] * _nn(om_ref[...], wb_ref[0:MOBA_W, :])
              + gates[:, d:2 * d] * _nn(od_ref[...], wb_ref[MOBA_W:MOBA_W + DIFF_W, :])
              + gates[:, 2 * d:3 * d] * _nn(os_ref[...], wb_ref[MOBA_W + DIFF_W:, :]))
    mix = _nn(merged.astype(BF16), wo_ref[...])
    y_ref[...] = _layer_norm(alpha * x + mix, g_ref[...], b_ref[...])


def _post(x2d, o_m, o_d, o_s, w_gate, w_branch, w_out, ln_g, ln_b, alpha, tm):
    m, d = x2d.shape
    row = lambda i: (i, 0)
    full = lambda a: pl.BlockSpec(a.shape, lambda i: (0, 0))
    return pl.pallas_call(
        functools.partial(_post_kernel, alpha=alpha),
        grid=(m // tm,),
        in_specs=[pl.BlockSpec((tm, d), row), pl.BlockSpec((tm, MOBA_W), row),
                  pl.BlockSpec((tm, DIFF_W), row), pl.BlockSpec((tm, SB_W), row),
                  full(w_gate), full(w_branch), full(w_out), full(ln_g), full(ln_b)],
        out_specs=pl.BlockSpec((tm, d), row),
        out_shape=jax.ShapeDtypeStruct((m, d), F32),
        compiler_params=_params("arbitrary"), name="post",
    )(x2d, o_m, o_d, o_s, w_gate, w_branch, w_out, ln_g, ln_b)


def _gelu(x):
    return 0.5 * x * (1.0 + lax.erf(x * (2.0 ** -0.5)))


FFN_CHUNKS = 2


def _ffn_chunks(dff):
    tiles = dff // LANES
    assert tiles * LANES == dff
    bounds = [LANES * ((tiles * i) // FFN_CHUNKS) for i in range(FFN_CHUNKS + 1)]
    return [(bounds[i], bounds[i + 1]) for i in range(FFN_CHUNKS) if bounds[i + 1] > bounds[i]]


def _ffn_kernel(*refs, alpha, tm, seq, has_past):
    if has_past:
        x_ref, ple_ref, p1_ref, p2_ref = refs[:4]
        refs = refs[4:]
    else:
        x_ref, ple_ref = refs[:2]
        refs = refs[2:]
    (wup_ref, cw_ref, cb_ref, wdn_ref, g_ref, b_ref, wple_ref, wpg_ref,
     y_ref, conv_ref, aext_ref) = refs
    dff = cw_ref.shape[1]
    ti = pl.program_id(1)
    x = x_ref[...]
    xb = x.astype(BF16)

    @pl.when(ti == 0)
    def _():
        aext_ref[0:SUBLANES, :] = jnp.zeros((SUBLANES, dff), F32)

    ffn = jnp.zeros(x.shape, F32)
    for c0, c1 in _ffn_chunks(dff):
        cs = slice(c0, c1)
        a = _nn(xb, wup_ref[:, c0:c1])
        gate_in = _nn(xb, wup_ref[:, dff + c0:dff + c1])
        aext_ref[SUBLANES:SUBLANES + tm, cs] = a
        a1 = aext_ref[SUBLANES - 1:SUBLANES - 1 + tm, cs]
        a2 = aext_ref[SUBLANES - 2:SUBLANES - 2 + tm, cs]
        if has_past:
            t = _iota((tm, 1), 0) % seq
            a1 = jnp.where(t >= 1, a1, p1_ref[:, cs])
            a2 = jnp.where(t >= 2, a2, p2_ref[:, cs])
            conv_ref[:, cs] = a
        else:
            aext_ref[0:SUBLANES, cs] = a[tm - SUBLANES:, :]
            conv_ref[:, cs] = a[tm - SUBLANES:, :]
        a_conv = cb_ref[:, cs] + (cw_ref[0:1, cs] * a2 + cw_ref[1:2, cs] * a1 + cw_ref[2:3, cs] * a)
        hid = _gelu(a_conv) * gate_in
        ffn = ffn + _nn(hid.astype(BF16), wdn_ref[c0:c1, :])
    y = _layer_norm(alpha * x + ffn, g_ref[...], b_ref[...])
    pg = _sigmoid(_nn(y.astype(BF16), wpg_ref[...]))
    y_ref[...] = y + pg * _nn(ple_ref[...].astype(BF16), wple_ref[...])


def _ffn(x2d, ple2d, past, w_up, conv_w, conv_b, w_down, ln_g, ln_b, w_ple, w_pg, alpha, tm, seq):
    m, d = x2d.shape
    dff = conv_w.shape[1]
    has_past = past is not None
    if has_past:
        assert m == tm and tm % seq == 0
        grid = (1, 1)
        nb = 1
    else:
        assert seq % tm == 0 and tm >= SUBLANES
        nb = seq // tm
        grid = (m // seq, nb)
    row = lambda i, j: (i * nb + j, 0)
    full = lambda a: pl.BlockSpec(a.shape, lambda i, j: (0,) * a.ndim)
    in_specs = [pl.BlockSpec((tm, d), row), pl.BlockSpec((tm, ple2d.shape[1]), row)]
    args = [x2d, ple2d]
    if has_past:
        in_specs += [pl.BlockSpec((tm, dff), row)] * 2
        args += list(past)
        conv_shape = jax.ShapeDtypeStruct((m, dff), F32)
        conv_spec = pl.BlockSpec((tm, dff), row)
    else:
        conv_shape = jax.ShapeDtypeStruct((m // seq, SUBLANES, dff), F32)
        conv_spec = pl.BlockSpec((None, SUBLANES, dff), lambda i, j: (i, 0, 0))
    weights = [w_up, conv_w, conv_b, w_down, ln_g, ln_b, w_ple, w_pg]
    in_specs += [full(w) for w in weights]
    return pl.pallas_call(
        functools.partial(_ffn_kernel, alpha=alpha, tm=tm, seq=seq, has_past=has_past),
        grid=grid, in_specs=in_specs,
        out_specs=[pl.BlockSpec((tm, d), row), conv_spec],
        out_shape=[jax.ShapeDtypeStruct((m, d), F32), conv_shape],
        scratch_shapes=[pltpu.VMEM((tm + SUBLANES, dff), F32)],
        compiler_params=_params("arbitrary", "arbitrary"), name="ffn",
    )(*args, *weights)


def kernel(x_prompt, x_sample, cache_k, cache_v, state_conv, page_table, p_prompt, p_sample, w_in, w_gate, w_branch, w_out, diff_lambda, diff_subln_g, ln1_g, ln1_b, w_up, conv_w, conv_b, w_down, ln2_g, ln2_b, w_ple, w_ple_gate):
    depth = w_in.shape[0]
    alpha = (2.0 * depth) ** 0.25
    bp, tp, d = x_prompt.shape
    bs, ts, _ = x_sample.shape
    dff = conv_w.shape[2]
    tm_p = 512 if tp % 512 == 0 else MOBA_BLOCK
    tm_f = 256
    assert tp % MOBA_BLOCK == 0
    yp = x_prompt.reshape(bp * tp, d)
    ys = x_sample.reshape(bs * ts, d)
    outs = {k: [] for k in ("cp", "cs")}
    kv_p, kv_s = (), ()
    row2 = lambda a: a.reshape(1, -1)
    for i in range(depth):
        lam_init = 0.8 - 0.6 * math.exp(-0.3 * i)
        wi, wg, wbr, wo = (w[i].astype(BF16) for w in (w_in, w_gate, w_branch, w_out))
        wu, wd, wpl, wpg = (w[i].astype(BF16) for w in (w_up, w_down, w_ple, w_ple_gate))
        lam_p, subg = diff_lambda[i], row2(diff_subln_g[i])
        l1g, l1b, l2g, l2b, cb = (row2(a[i]) for a in (ln1_g, ln1_b, ln2_g, ln2_b, conv_b))
        cw = conv_w[i]

        *kv_p, qb, kb, vb, qmf, kmean = _proj(yp, wi, tm_p, True, i, depth, tuple(kv_p))
        to3 = lambda a: a.reshape(bp, tp, a.shape[-1])
        qb3, kb3, vb3 = to3(qb), to3(kb), to3(vb)
        o_m = _moba(qb3, to3(qmf), kmean.reshape(bp, tp // MOBA_BLOCK, MOBA_W), kb3, vb3)
        o_d = _diff(qb3, kb3, vb3, lam_p, subg.reshape(-1, 1), lam_init)
        o_s = _sb(qb3, kb3, vb3)
        x1 = _post(yp, o_m.reshape(-1, MOBA_W), o_d.reshape(-1, DIFF_W), o_s.reshape(-1, SB_W),
                   wg, wbr, wo, l1g, l1b, alpha, tm_f)
        yp, conv = _ffn(x1, p_prompt[i].reshape(bp * tp, -1), None, wu, cw, cb, wd, l2g, l2b, wpl, wpg,
                        alpha, tm_f, tp)
        outs["cp"].append(conv[:, SUBLANES - (CONV_W - 1):, :])

        ms = bs * ts
        *kv_s, qb, kb, vb, qmf = _proj(ys, wi, ms, False, i, depth, tuple(kv_s))
        to3 = lambda a: a.reshape(bs, ts, a.shape[-1])
        o = _sample_attn(i, to3(qb), to3(qmf), to3(kb), to3(vb), cache_k, cache_v, page_table,
                         lam_p, subg, lam_init).reshape(ms, -1).astype(BF16)
        x1 = _post(ys, o[:, :MOBA_W], o[:, MOBA_W:MOBA_W + DIFF_W], o[:, MOBA_W + DIFF_W:],
                   wg, wbr, wo, l1g, l1b, alpha, ms)
        st = state_conv[i]
        zeros = jnp.zeros((bs, ts - 1, dff), F32)
        p1 = jnp.concatenate([st[:, 1:2], zeros], axis=1).reshape(ms, dff)
        p2 = jnp.concatenate([st, zeros[:, 1:]], axis=1).reshape(ms, dff)
        ys, a_full = _ffn(x1, p_sample[i].reshape(ms, -1), (p1, p2), wu, cw, cb, wd, l2g, l2b, wpl, wpg,
                          alpha, ms, ts)
        outs["cs"].append(a_full.reshape(bs, ts, dff)[:, ts - (CONV_W - 1):, :])
    st = lambda name: jnp.stack(outs[name])
    k_p, v_p = (a.reshape(depth, bp, tp, -1) for a in kv_p)
    k_s, v_s = (a.reshape(depth, bs, ts, -1) for a in kv_s)
    return (yp.reshape(bp, tp, d), ys.reshape(bs, ts, d), k_p, v_p, st("cp"), k_s, v_s, st("cs"))
```

```python
import functools
import math

import numpy as np
import jax
import jax.numpy as jnp
from jax import lax
from jax.experimental import pallas as pl
from jax.experimental.pallas import tpu as pltpu

F32 = jnp.float32
BF16 = jnp.bfloat16

HEAD_DIM = 64
MOBA_HEADS = 4
MOBA_BLOCK = 256
MOBA_TOPK = 3
DIFF_HEADS = 4
SB_HEADS = 4
MOBA_W = MOBA_HEADS * HEAD_DIM
DIFF_W = DIFF_HEADS * 2 * HEAD_DIM
SB_W = SB_HEADS * HEAD_DIM
QK_W = MOBA_W + DIFF_W + SB_W
N_GROUPS = QK_W // HEAD_DIM
N_BRANCH = 3
N_ALIBI = MOBA_HEADS + DIFF_HEADS
CONV_W = 3
LN_EPS = 1e-5
QK_SCALE = HEAD_DIM ** -0.5
NEG = -1e30
LANES = 128
SUBLANES = 8
VMEM_LIMIT = 56 * 1024 * 1024
TQ = MOBA_BLOCK
GATE_ROWS = 2 * SUBLANES
DECODE_BLOCKS_PER_STEP = 4
DECODE_SLOTS = 3
SEQS_PER_STEP = 2
DIFF_HEADS_PER_CHAIN = 2
ONES_ROWS = 2 * SUBLANES
LOG2E = math.log2(math.e)

_SLOPES = [2.0 ** (-8.0 * h / N_ALIBI) for h in range(1, N_ALIBI + 1)]
MOBA_SLOPES = _SLOPES[0::2]
DIFF_SLOPES = _SLOPES[1::2]


def _nt(a, b):
    return lax.dot_general(a, b, (((1,), (1,)), ((), ())), preferred_element_type=F32)


def _nn(a, b):
    return jnp.dot(a, b, preferred_element_type=F32)


def _split(x):
    hi = x.astype(BF16)
    lo = (x - hi.astype(F32)).astype(BF16)
    return hi, lo


def _sigmoid(x):
    return 1.0 / (1.0 + jnp.exp(-x))


def _softplus(z):
    return jnp.maximum(z, 0.0) + jnp.log(1.0 + jnp.exp(-jnp.abs(z)))


def _layer_norm(x, g, b):
    mu = jnp.mean(x, axis=-1, keepdims=True)
    xc = x - mu
    var = jnp.mean(xc * xc, axis=-1, keepdims=True)
    return xc * lax.rsqrt(var + LN_EPS) * g + b


def _params(*sem):
    return pltpu.CompilerParams(dimension_semantics=sem, vmem_limit_bytes=VMEM_LIMIT)


def _iota(shape, dim):
    return lax.broadcasted_iota(jnp.int32, shape, dim)


def _proj_kernel(*refs, n_blk, n_carried):
    x_ref, w_ref = refs[:2]
    k_ref, v_ref, qb_ref, kb_ref, vb_ref, qmf_ref, *rest = refs[2 + n_carried:]
    xb = x_ref[...].astype(BF16)
    q = _nn(xb, w_ref[:, 0:QK_W])
    k = _nn(xb, w_ref[:, QK_W:2 * QK_W])
    v = _nn(xb, w_ref[:, 2 * QK_W:3 * QK_W])
    if n_carried:
        k_ref[...] = k
        v_ref[...] = v
    else:
        k_ref[0] = k
        v_ref[0] = v
        for later in range(1, k_ref.shape[0]):
            k_ref[later] = jnp.zeros_like(k)
            v_ref[later] = jnp.zeros_like(v)
    qb_ref[...] = (q * QK_SCALE).astype(BF16)
    kb_ref[...] = k.astype(BF16)
    vb_ref[...] = v.astype(BF16)
    qmf_ref[...] = q[:, :MOBA_W]
    if n_blk:
        kmean_ref = rest[0]
        for j in range(n_blk):
            blk = k[j * MOBA_BLOCK:(j + 1) * MOBA_BLOCK, :MOBA_W]
            kmean_ref[j:j + 1, :] = jnp.sum(blk, axis=0, keepdims=True) * (1.0 / MOBA_BLOCK)


def _proj(x2d, w, tm, with_kmean, layer, depth, kv_all=()):
    m, d = x2d.shape
    n_blk = tm // MOBA_BLOCK if with_kmean else 0
    row = lambda i: (i, 0)
    out_shape = [jax.ShapeDtypeStruct((depth, m, QK_W), F32), jax.ShapeDtypeStruct((depth, m, QK_W), F32),
                 jax.ShapeDtypeStruct((m, QK_W), BF16), jax.ShapeDtypeStruct((m, QK_W), BF16),
                 jax.ShapeDtypeStruct((m, QK_W), BF16), jax.ShapeDtypeStruct((m, MOBA_W), F32)]
    assert bool(kv_all) == (layer > 0)
    kv_spec = (pl.BlockSpec((None, tm, QK_W), lambda i: (layer, i, 0)) if kv_all
               else pl.BlockSpec((depth, tm, QK_W), lambda i: (0, i, 0)))
    out_specs = ([kv_spec] * 2
                 + [pl.BlockSpec((tm, QK_W), row)] * 3 + [pl.BlockSpec((tm, MOBA_W), row)])
    if with_kmean:
        out_shape.append(jax.ShapeDtypeStruct((m // tm, n_blk, MOBA_W), F32))
        out_specs.append(pl.BlockSpec((None, n_blk, MOBA_W), lambda i: (i, 0, 0)))
    in_specs = [pl.BlockSpec((tm, d), row), pl.BlockSpec(w.shape, lambda i: (0, 0))]
    in_specs += [pl.BlockSpec(memory_space=pl.ANY)] * len(kv_all)
    return pl.pallas_call(
        functools.partial(_proj_kernel, n_blk=n_blk, n_carried=len(kv_all)),
        grid=(m // tm,),
        in_specs=in_specs, out_specs=out_specs, out_shape=out_shape,
        input_output_aliases={2 + i: i for i in range(len(kv_all))},
        compiler_params=_params("arbitrary"), name="proj",
    )(x2d, w, *kv_all)


def _fill_vt(v_ref, vt_ref, n_heads, width):
    t = v_ref.shape[0]
    extra = vt_ref.shape[1] - width
    for h in range(n_heads):
        if extra:
            vt_ref[h, width:, :] = jnp.ones((extra, t), BF16)
    for j in range(t // TQ):
        blk = v_ref[j * TQ:(j + 1) * TQ, :].astype(F32).T.astype(BF16)
        for h in range(n_heads):
            vt_ref[h, :width, j * TQ:(j + 1) * TQ] = blk[h * width:(h + 1) * width, :]


def _stack_heads_t(q_bf16, n_heads, width):
    qt = q_bf16.astype(F32).T * LOG2E
    row_head = _iota((n_heads * width, 1), 0) // width
    return jnp.concatenate([jnp.where(row_head == h, qt, 0.0).astype(BF16) for h in range(n_heads)], axis=1)


def _lane_slopes(slopes, n_lanes):
    lane_head = _iota((1, n_lanes), 1) // TQ
    out = jnp.zeros((1, n_lanes), F32)
    for h, s in enumerate(slopes):
        out = jnp.where(lane_head == h, s, out)
    return out


def _softmax_step(s, m, extra):
    cand = jnp.max(s, axis=0, keepdims=True) + extra
    m_new = jnp.maximum(m, cand)
    alpha = jnp.exp2(m - m_new)
    p = jnp.exp2(s - (m_new - extra))
    return p.astype(BF16), alpha, m_new


def _moba_kernel(q_ref, qf_ref, kmean_ref, k_ref, v_ref, o_ref, vt_ref, bias_ref, selb_ref, acc_ref, *, n_blk):
    nq = MOBA_HEADS * TQ
    qi = pl.program_id(1)

    slope_row = _lane_slopes(MOBA_SLOPES, nq) * LOG2E

    @pl.when(qi == 0)
    def _():
        _fill_vt(v_ref, vt_ref, MOBA_HEADS, HEAD_DIM)
        bias_ref[...] = _iota((TQ, nq), 0).astype(F32) * slope_row

    qt_all = _stack_heads_t(q_ref[...], MOBA_HEADS, HEAD_DIM)

    qft_hi, qft_lo = _split(qf_ref[...].T)
    km = kmean_ref[...]
    km = jnp.concatenate([km, jnp.zeros((GATE_ROWS - n_blk, MOBA_W), F32)], axis=0)
    lane_head = _iota((1, MOBA_W), 1) // HEAD_DIM
    blk_row = _iota((GATE_ROWS, 1), 0)
    for h in range(MOBA_HEADS):
        kh_hi, kh_lo = _split(jnp.where(lane_head == h, km, 0.0))
        g = _nn(kh_hi, qft_hi) + _nn(kh_hi, qft_lo) + _nn(kh_lo, qft_hi)
        cnt = jnp.zeros((GATE_ROWS, TQ), F32)
        for mb in range(n_blk - 1):
            gm = g[mb:mb + 1, :]
            tie = jnp.where(blk_row > mb, 1.0, 0.0)
            beats = jnp.where(gm > g, 1.0, jnp.where(gm == g, tie, 0.0))
            cnt = cnt + beats * jnp.where(mb < qi, 1.0, 0.0)
        sel = jnp.where(blk_row < qi, jnp.where(cnt < MOBA_TOPK, 0.0, NEG), NEG)
        selb_ref[:, h * TQ:(h + 1) * TQ] = sel

    acc_ref[...] = jnp.zeros_like(acc_ref)

    def scores(n, diag):
        s = _nn(k_ref[pl.ds(pl.multiple_of(n * TQ, TQ), TQ), :], qt_all) + bias_ref[...]
        if diag:
            s = jnp.where(_iota((TQ, nq), 0) <= _iota((TQ, nq), 1) % TQ, s, NEG)
        return s

    def attend(s, n, diag, m):
        start = pl.multiple_of(n * TQ, TQ)
        extra = slope_row * ((n - qi) * TQ).astype(F32)
        if not diag:
            extra = extra + selb_ref[pl.ds(n, 1), :]
        pb, alpha, m = _softmax_step(s, m, extra)
        for h in range(MOBA_HEADS):
            cols = slice(h * TQ, (h + 1) * TQ)
            pv = _nn(vt_ref[h, :, pl.ds(start, TQ)], pb[:, cols])
            acc_ref[h] = alpha[:, cols] * acc_ref[h] + pv
        return m

    m = attend(scores(qi, True), qi, True, jnp.full((1, nq), NEG, F32))
    lax.fori_loop(0, qi, lambda n, m: attend(scores(n, False), n, False, m), m)
    outs = []
    for h in range(MOBA_HEADS):
        acc = acc_ref[h]
        outs.append(acc[:HEAD_DIM] * (1.0 / acc[HEAD_DIM:HEAD_DIM + 1]))
    o_ref[...] = jnp.concatenate(outs, axis=0).T.astype(o_ref.dtype)


def _moba(qb, qmf, kmean, kb, vb):
    b, t, _ = qb.shape
    n_blk = t // TQ
    assert n_blk <= GATE_ROWS
    nq = MOBA_HEADS * TQ
    return pl.pallas_call(
        functools.partial(_moba_kernel, n_blk=n_blk),
        grid=(b, n_blk),
        in_specs=[pl.BlockSpec((None, TQ, MOBA_W), lambda i, j: (i, j, 0)),
                  pl.BlockSpec((None, TQ, MOBA_W), lambda i, j: (i, j, 0)),
                  pl.BlockSpec((None, n_blk, MOBA_W), lambda i, j: (i, 0, 0)),
                  pl.BlockSpec((None, t, MOBA_W), lambda i, j: (i, 0, 0)),
                  pl.BlockSpec((None, t, MOBA_W), lambda i, j: (i, 0, 0))],
        out_specs=pl.BlockSpec((None, TQ, MOBA_W), lambda i, j: (i, j, 0)),
        out_shape=jax.ShapeDtypeStruct((b, t, MOBA_W), BF16),
        scratch_shapes=[pltpu.VMEM((MOBA_HEADS, HEAD_DIM + ONES_ROWS, t), BF16), pltpu.VMEM((TQ, nq), F32),
                        pltpu.VMEM((GATE_ROWS, nq), F32),
                        pltpu.VMEM((MOBA_HEADS, HEAD_DIM + ONES_ROWS, TQ), F32)],
        compiler_params=_params("arbitrary", "arbitrary"), name="moba",
    )(qb, qmf, kmean, kb, vb)


def _suffix_matrix(n, transposed):
    r = _iota((n, n), 0)
    c = _iota((n, n), 1)
    return jnp.where((c > r) if transposed else (r > c), 1.0, 0.0).astype(BF16)


def _sb_kernel(q_ref, k_ref, v_ref, o_ref, vt_ref, acc_ref):
    nq = SB_HEADS * TQ
    nc = q_ref.shape[0]
    qi = pl.program_id(1)

    @pl.when(qi == 0)
    def _():
        for c in range(nc):
            _fill_vt(v_ref.at[c], vt_ref.at[c], SB_HEADS, HEAD_DIM)

    qt_all = [_stack_heads_t(q_ref[c], SB_HEADS, HEAD_DIM) for c in range(nc)]
    r = _iota((TQ + ONES_ROWS, 2 * TQ), 0)
    c = _iota((TQ + ONES_ROWS, 2 * TQ), 1) % TQ
    later_mat = jnp.where(r >= TQ, -1.0, jnp.where(c > r, -1.0, 0.0)).astype(BF16)
    acc_ref[...] = jnp.zeros_like(acc_ref)

    def scores(c, n):
        return _nn(k_ref[c, pl.ds(pl.multiple_of(n * TQ, TQ), TQ), :], qt_all[c])

    def attend(c, z, n, diag, cs):
        start = pl.multiple_of(n * TQ, TQ)
        sp = jnp.maximum(z, 0.0) + jnp.log(1.0 + jnp.exp2(-jnp.abs(z))) * LOG2E
        if diag:
            before = _iota((TQ, nq), 0) < _iota((TQ, nq), 1) % TQ
            sp_keep = jnp.where(before, sp, 0.0)
        else:
            sp_keep = sp
        sums = _nn(later_mat, jnp.concatenate(_split(sp_keep), axis=0))
        a = jnp.exp2((z - sp) + sums[:TQ] + cs)
        if diag:
            a = jnp.where(before, a, 0.0)
        ab = a.astype(BF16)
        for h in range(SB_HEADS):
            acc_ref[c, h] += _nn(vt_ref[c, h, :, pl.ds(start, TQ)], ab[:, h * TQ:(h + 1) * TQ])
        return cs + sums[TQ:TQ + 1]

    def step(n, diag, css):
        return tuple(attend(c, scores(c, n), n, diag, css[c]) for c in range(nc))

    css = step(qi, True, tuple(jnp.zeros((1, nq), F32) for _ in range(nc)))
    lax.fori_loop(0, qi, lambda i, css: step(qi - 1 - i, False, css), css)
    for c in range(nc):
        o_ref[c] = jnp.concatenate([acc_ref[c, h] for h in range(SB_HEADS)], axis=0).T.astype(o_ref.dtype)


def _sb(qb, kb, vb):
    b, t, _ = qb.shape
    cb = (QK_W - SB_W) // SB_W
    nc = math.gcd(b, SEQS_PER_STEP)
    return pl.pallas_call(
        _sb_kernel,
        grid=(b // nc, t // TQ),
        in_specs=[pl.BlockSpec((nc, TQ, SB_W), lambda i, j: (i, j, cb)),
                  pl.BlockSpec((nc, t, SB_W), lambda i, j: (i, 0, cb)),
                  pl.BlockSpec((nc, t, SB_W), lambda i, j: (i, 0, cb))],
        out_specs=pl.BlockSpec((nc, TQ, SB_W), lambda i, j: (i, j, 0)),
        out_shape=jax.ShapeDtypeStruct((b, t, SB_W), BF16),
        scratch_shapes=[pltpu.VMEM((nc, SB_HEADS, HEAD_DIM, t), BF16),
                        pltpu.VMEM((nc, SB_HEADS, HEAD_DIM, TQ), F32)],
        compiler_params=_params("arbitrary", "arbitrary"), name="sb",
    )(qb, kb, vb)


def _diff_lambda(lam_ref, lam_init):
    dl = lam_ref[...]
    s1 = jnp.sum(dl[0:1] * dl[1:2], axis=1, keepdims=True)
    s2 = jnp.sum(dl[2:3] * dl[3:4], axis=1, keepdims=True)
    return jnp.exp(s1) - jnp.exp(s2) + lam_init


def _diff_kernel(*refs, lam_init):
    hd = 2 * HEAD_DIM
    nh = DIFF_HEADS_PER_CHAIN
    nc = DIFF_HEADS // nh
    nq = nh * 2 * TQ
    q_refs, k_refs, v_refs = refs[:nc], refs[nc:2 * nc], refs[2 * nc:3 * nc]
    lam_ref, g_ref, o_ref, vt_ref, bias_ref, acc_ref = refs[3 * nc:]
    qi = pl.program_id(1)
    slope_rows = []
    for c in range(nc):
        lane_head = _iota((1, nq), 1) // (2 * TQ) + c * nh
        row = jnp.zeros((1, nq), F32)
        for i in range(DIFF_HEADS):
            row = jnp.where(lane_head == i, DIFF_SLOPES[i] * LOG2E, row)
        slope_rows.append(row)

    @pl.when(qi == 0)
    def _():
        for c in range(nc):
            _fill_vt(v_refs[c], vt_ref.at[c], nh, hd)
            bias_ref[c] = _iota((TQ, nq), 0).astype(F32) * slope_rows[c]

    qt_all = [_stack_heads_t(q_refs[c][...], 2 * nh, HEAD_DIM) for c in range(nc)]
    acc_ref[...] = jnp.zeros_like(acc_ref)

    def scores(c, n, diag):
        s = _nn(k_refs[c][pl.ds(pl.multiple_of(n * TQ, TQ), TQ), :], qt_all[c]) + bias_ref[c]
        if diag:
            s = jnp.where(_iota((TQ, nq), 0) <= _iota((TQ, nq), 1) % TQ, s, NEG)
        return s

    def attend(c, s, n, m):
        start = pl.multiple_of(n * TQ, TQ)
        extra = slope_rows[c] * ((n - qi) * TQ).astype(F32)
        pb, alpha, m = _softmax_step(s, m, extra)
        for h in range(nh):
            cols = slice(h * 2 * TQ, (h + 1) * 2 * TQ)
            pv = _nn(vt_ref[c, h, :, pl.ds(start, TQ)], pb[:, cols])
            acc_ref[c, h] = alpha[:, cols] * acc_ref[c, h] + pv
        return m

    def step(n, diag, ms):
        return tuple(attend(c, scores(c, n, diag), n, ms[c]) for c in range(nc))

    ms = step(qi, True, tuple(jnp.full((1, nq), NEG, F32) for _ in range(nc)))
    lax.fori_loop(0, qi, lambda n, ms: step(n, False, ms), ms)
    lam = _diff_lambda(lam_ref, lam_init)
    outs = []
    for c in range(nc):
        for h in range(nh):
            acc = acc_ref[c, h]
            acc = acc[:hd] * (1.0 / acc[hd:hd + 1])
            o = acc[:, :TQ] - lam * acc[:, TQ:]
            ms2 = jnp.mean(o * o, axis=0, keepdims=True)
            outs.append(o * lax.rsqrt(ms2 + LN_EPS) * g_ref[...] * (1.0 - lam_init))
    o_ref[...] = jnp.concatenate(outs, axis=0).T.astype(o_ref.dtype)


def _diff(qb, kb, vb, lam_p, subln_g_col, lam_init):
    b, t, _ = qb.shape
    nh = DIFF_HEADS_PER_CHAIN
    nc = DIFF_HEADS // nh
    w = nh * 2 * HEAD_DIM
    assert MOBA_W % w == 0 and DIFF_HEADS % nh == 0
    c0 = MOBA_W // w
    tile = lambda c: pl.BlockSpec((None, TQ, w), lambda i, j: (i, j, c0 + c))
    seq = lambda c: pl.BlockSpec((None, t, w), lambda i, j: (i, 0, c0 + c))
    rows = 2 * HEAD_DIM + ONES_ROWS
    return pl.pallas_call(
        functools.partial(_diff_kernel, lam_init=lam_init),
        grid=(b, t // TQ),
        in_specs=([tile(c) for c in range(nc)] + [seq(c) for c in range(nc)] * 2
                  + [pl.BlockSpec(lam_p.shape, lambda i, j: (0, 0)),
                     pl.BlockSpec(subln_g_col.shape, lambda i, j: (0, 0))]),
        out_specs=pl.BlockSpec((None, TQ, DIFF_W), lambda i, j: (i, j, 0)),
        out_shape=jax.ShapeDtypeStruct((b, t, DIFF_W), BF16),
        scratch_shapes=[pltpu.VMEM((nc, nh, rows, t), BF16),
                        pltpu.VMEM((nc, TQ, nh * 2 * TQ), F32),
                        pltpu.VMEM((nc, nh, rows, 2 * TQ), F32)],
        compiler_params=_params("arbitrary", "arbitrary"), name="diff",
    )(*([qb] * nc), *([kb] * nc), *([vb] * nc), lam_p, subln_g_col)


def _sample_attn_kernel(pt_ref, qbd_ref, qg_ref, kn_ref, vn_ref, lam_ref, g_ref, ck_ref, cv_ref, o_ref,
                        acc_ref, m_ref, l_ref, cs_ref, gall_ref, mall_ref, lall_ref, oall_ref,
                        kbuf_ref, vbuf_ref, sem_ref, *,
                        layer, n_blk, tq, past_len, lam_init, n_pages_step, blocks_per_step):
    rows = tq * N_GROUPS
    b = pl.program_id(0)
    j = pl.program_id(1)
    last = pl.num_programs(1) - 1
    page = kbuf_ref.shape[2]
    ppb = MOBA_BLOCK // page
    n_chunks = n_blk // blocks_per_step
    total_chunks = pl.num_programs(0) * n_chunks

    def chunk_copies(c):
        slot = c % DECODE_SLOTS
        seq = c // n_chunks
        first_page = (n_blk - (c % n_chunks + 1) * blocks_per_step) * ppb
        copies = []
        for p in range(n_pages_step):
            pid = pt_ref[seq, first_page + p]
            copies.append(pltpu.make_async_copy(ck_ref.at[layer, pid], kbuf_ref.at[slot, p],
                                                sem_ref.at[slot, p]))
            copies.append(pltpu.make_async_copy(cv_ref.at[layer, pid], vbuf_ref.at[slot, p],
                                                sem_ref.at[slot, n_pages_step + p]))
        return copies

    def start_chunk(c):
        @pl.when(c < total_chunks)
        def _():
            for cp in chunk_copies(c):
                cp.start()

    @pl.when((b == 0) & (j == 0))
    def _():
        for c in range(DECODE_SLOTS - 1):
            start_chunk(c)

    row = _iota((rows, 1), 0)
    grp = row % N_GROUPS
    qidx = row // N_GROUPS
    kind = jnp.where(grp < MOBA_HEADS, 0, jnp.where(grp < MOBA_HEADS + 2 * DIFF_HEADS, 1, 2))
    is_moba = kind == 0
    is_diff = kind == 1
    is_sb = kind == 2
    slope = jnp.zeros((rows, 1), F32)
    for h in range(MOBA_HEADS):
        slope = jnp.where(grp == h, MOBA_SLOPES[h], slope)
    for h in range(DIFF_HEADS):
        slope = jnp.where(grp - MOBA_HEADS - 2 * h == 0, DIFF_SLOPES[h], slope)
        slope = jnp.where(grp - MOBA_HEADS - 2 * h == 1, DIFF_SLOPES[h], slope)
    blk_lane = _iota((rows, LANES), 1)

    @pl.when(j == 0)
    def _():
        acc_ref[...] = jnp.zeros_like(acc_ref)
        m_ref[...] = jnp.full_like(m_ref, NEG)
        l_ref[...] = jnp.zeros_like(l_ref)
        cs_ref[...] = jnp.zeros_like(cs_ref)
        gall_ref[...] = jnp.zeros_like(gall_ref)
        mall_ref[...] = jnp.full_like(mall_ref, NEG)
        lall_ref[...] = jnp.zeros_like(lall_ref)

    def process(kb, vb, seg, n_first, pos0, limit):
        nk = kb.shape[0]
        n_seg = nk // seg
        s_raw = _nt(qbd_ref[...], kb)
        col = _iota((1, nk), 1)
        s = s_raw + slope * (col.astype(F32) + pos0)
        if limit is not None:
            valid = col < limit
            s = jnp.where(valid, s, NEG)
        segs = [slice(i * seg, (i + 1) * seg) for i in range(n_seg)]
        seg_max = [jnp.max(s[:, sl], axis=1, keepdims=True) for sl in segs]
        m_old = m_ref[...]
        m_new = m_old
        for sm in seg_max:
            m_new = jnp.maximum(m_new, sm)
        alpha = jnp.exp(m_old - m_new)
        sp = _softplus(s_raw)
        lk = -sp
        if limit is not None:
            lk = jnp.where(valid, lk, 0.0)
        later_mat = _suffix_matrix(seg, False)
        run = cs_ref[...]
        a_segs = [None] * n_seg
        for i in reversed(range(n_seg)):
            lk_hi, lk_lo = _split(lk[:, segs[i]])
            later = _nn(lk_hi, later_mat) + _nn(lk_lo, later_mat) + run
            a_segs[i] = jnp.exp((s_raw[:, segs[i]] - sp[:, segs[i]]) + later)
            run = run + jnp.sum(lk[:, segs[i]], axis=1, keepdims=True)
        cs_ref[...] = run
        psum = jnp.zeros((rows, 1), F32)
        p_segs = []
        mall, lall = mall_ref[...], lall_ref[...]
        for i in range(n_seg):
            p = jnp.exp(s[:, segs[i]] - jnp.where(is_moba, seg_max[i], m_new))
            ps = jnp.sum(p, axis=1, keepdims=True)
            psum = psum + ps
            mall = jnp.where(blk_lane == n_first + i, seg_max[i], mall)
            lall = jnp.where(blk_lane == n_first + i, ps, lall)
            a = a_segs[i]
            if limit is not None:
                a = jnp.where(valid[:, segs[i]], a, 0.0)
            p_segs.append(jnp.where(is_sb, a, p).astype(BF16))
        mall_ref[...] = mall
        lall_ref[...] = lall
        l_ref[...] = jnp.where(is_diff, alpha * l_ref[...] + psum, l_ref[...])
        m_ref[...] = jnp.where(is_diff, m_new, m_old)
        for i in range(n_seg):
            oall_ref[n_first + i] = _nn(p_segs[i], vb[segs[i], :MOBA_W])
        pmat = p_segs[0] if n_seg == 1 else jnp.concatenate(p_segs, axis=1)
        fac = jnp.where(is_diff, alpha, jnp.where(is_sb, 1.0, 0.0))
        acc_ref[...] = acc_ref[...] * fac + _nn(pmat, vb[:, MOBA_W:])

    @pl.when(j == 0)
    def _():
        limit = jnp.where(is_sb, qidx, qidx + 1)
        process(kn_ref[...], vn_ref[...], kn_ref.shape[0], n_blk, 0.0, limit)

    @pl.when(j > 0)
    def _():
        chunk = b * n_chunks + j - 1
        for cp in chunk_copies(chunk):
            cp.wait()
        start_chunk(chunk + DECODE_SLOTS - 1)
        slot = chunk % DECODE_SLOTS
        n_first = n_blk - j * blocks_per_step
        gall = gall_ref[...]
        for i in range(blocks_per_step):
            ksum = jnp.zeros((1, MOBA_W), F32)
            for p in range(ppb):
                ksum = ksum + jnp.sum(kbuf_ref[slot, i * ppb + p, :, :MOBA_W], axis=0, keepdims=True)
            gate = jnp.sum(qg_ref[...] * (ksum * (1.0 / MOBA_BLOCK)), axis=1, keepdims=True)
            gall = jnp.where(blk_lane == n_first + i, gate, gall)
        gall_ref[...] = gall
        kb = jnp.concatenate([kbuf_ref[slot, p].astype(BF16) for p in range(n_pages_step)], axis=0)
        vb = jnp.concatenate([vbuf_ref[slot, p].astype(BF16) for p in range(n_pages_step)], axis=0)
        pos0 = (n_first * MOBA_BLOCK - past_len).astype(F32)
        process(kb, vb, MOBA_BLOCK, n_first, pos0, None)

    @pl.when(j == last)
    def _():
        g = gall_ref[...]
        cnt = jnp.zeros((rows, LANES), F32)
        for mb in range(n_blk):
            gm = g[:, mb:mb + 1]
            tie = jnp.where(blk_lane > mb, 1.0, 0.0)
            cnt = cnt + jnp.where(gm > g, 1.0, jnp.where(gm == g, tie, 0.0))
        rank_lim = jnp.where(blk_lane < n_blk, float(MOBA_TOPK), jnp.where(blk_lane == n_blk, 1e9, -1.0))
        sel = cnt < rank_lim
        mall = jnp.where(sel, mall_ref[...], NEG)
        m_tot = jnp.max(mall, axis=1, keepdims=True)
        w = jnp.where(sel, jnp.exp(mall - m_tot), 0.0)
        l_tot = jnp.sum(w * lall_ref[...], axis=1, keepdims=True)
        o_m = jnp.zeros((rows, MOBA_W), F32)
        for nb in range(n_blk + 1):
            o_m = o_m + w[:, nb:nb + 1] * oall_ref[nb]
        o_m = o_m * jnp.where(is_moba, 1.0 / l_tot, 0.0)
        rest = acc_ref[...] * jnp.where(is_diff, 1.0 / l_ref[...], jnp.where(is_sb, 1.0, 0.0))
        acc = jnp.concatenate([o_m, rest], axis=1)
        lam = _diff_lambda(lam_ref, lam_init)
        g16 = _iota((N_GROUPS, 1), 0)
        cgrp = _iota((1, QK_W), 1) // HEAD_DIM
        d_lo, d_hi = MOBA_HEADS, MOBA_HEADS + 2 * DIFF_HEADS
        g_diff = jnp.where(g16 >= d_lo, jnp.where(g16 < d_hi, 1, 0), 0)
        c_diff = jnp.where(cgrp >= d_lo, jnp.where(cgrp < d_hi, 1, 0), 0)
        g_key = jnp.where(g_diff == 1, d_lo + 2 * ((g16 - d_lo) // 2), g16)
        c_key = jnp.where(c_diff == 1, d_lo + 2 * ((cgrp - d_lo) // 2), cgrp)
        second_half = g_diff * ((g16 - d_lo) % 2)
        coef = jnp.where(g_key == c_key, jnp.where(second_half == 1, -lam, 1.0), 0.0)
        hd = 2 * HEAD_DIM
        gsub = g_ref[...]
        for t in range(tq):
            o_t = jnp.sum(acc[t * N_GROUPS:(t + 1) * N_GROUPS, :] * coef, axis=0, keepdims=True)
            pieces = [o_t[:, :MOBA_W]]
            for h in range(DIFF_HEADS):
                od = o_t[:, MOBA_W + h * hd:MOBA_W + (h + 1) * hd]
                ms2 = jnp.mean(od * od, axis=1, keepdims=True)
                pieces.append(od * lax.rsqrt(ms2 + LN_EPS) * gsub * (1.0 - lam_init))
            pieces.append(o_t[:, MOBA_W + DIFF_W:])
            o_ref[t:t + 1, :] = jnp.concatenate(pieces, axis=1)


def _sample_attn(layer, qb, qmf, kb_new, vb_new, cache_k, cache_v, page_table, lam_p, subln_g, lam_init):
    b, tq, _ = qb.shape
    page = cache_k.shape[2]
    n_pages = page_table.shape[1]
    past_len = n_pages * page
    assert past_len % MOBA_BLOCK == 0 and MOBA_BLOCK % page == 0 and tq <= SUBLANES
    n_blk = past_len // MOBA_BLOCK
    assert n_blk + 1 <= LANES
    bps = math.gcd(n_blk, DECODE_BLOCKS_PER_STEP)
    pps = bps * (MOBA_BLOCK // page)
    rows = tq * N_GROUPS
    n_new = LANES

    gmask = (np.arange(QK_W)[None, :] // HEAD_DIM == np.arange(N_GROUPS)[:, None])
    qbd = jnp.where(gmask[None, None], qb[:, :, None, :], jnp.zeros((), BF16)).reshape(b, rows, QK_W)
    qg = jnp.where(gmask[None, None, :, :MOBA_W], qmf[:, :, None, :], 0.0).reshape(b, rows, MOBA_W)
    pad = ((0, 0), (0, n_new - tq), (0, 0))
    kn = jnp.pad(kb_new, pad)
    vn = jnp.pad(vb_new, pad)

    per_b = lambda i, j, pt: (i, 0, 0)
    whole = lambda i, j, pt: (0, 0)
    in_specs = [pl.BlockSpec((None, rows, QK_W), per_b),
                pl.BlockSpec((None, rows, MOBA_W), per_b),
                pl.BlockSpec((None, n_new, QK_W), per_b),
                pl.BlockSpec((None, n_new, QK_W), per_b),
                pl.BlockSpec(lam_p.shape, whole),
                pl.BlockSpec(subln_g.shape, whole),
                pl.BlockSpec(memory_space=pl.ANY),
                pl.BlockSpec(memory_space=pl.ANY)]
    grid_spec = pltpu.PrefetchScalarGridSpec(
        num_scalar_prefetch=1, grid=(b, n_blk // bps + 1), in_specs=in_specs,
        out_specs=pl.BlockSpec((None, tq, QK_W), per_b),
        scratch_shapes=[pltpu.VMEM((rows, QK_W - MOBA_W), F32), pltpu.VMEM((rows, 1), F32),
                        pltpu.VMEM((rows, 1), F32), pltpu.VMEM((rows, 1), F32),
                        pltpu.VMEM((rows, LANES), F32), pltpu.VMEM((rows, LANES), F32),
                        pltpu.VMEM((rows, LANES), F32), pltpu.VMEM((n_blk + 1, rows, MOBA_W), F32),
                        pltpu.VMEM((DECODE_SLOTS, pps, page, QK_W), F32),
                        pltpu.VMEM((DECODE_SLOTS, pps, page, QK_W), F32),
                        pltpu.SemaphoreType.DMA((DECODE_SLOTS, 2 * pps))])
    return pl.pallas_call(
        functools.partial(_sample_attn_kernel, layer=layer, n_blk=n_blk, tq=tq, past_len=past_len,
                          lam_init=lam_init, n_pages_step=pps, blocks_per_step=bps),
        grid_spec=grid_spec,
        out_shape=jax.ShapeDtypeStruct((b, tq, QK_W), F32),
        compiler_params=_params("arbitrary", "arbitrary"), name="sample_attn",
    )(page_table, qbd, qg, kn, vn, lam_p, subln_g, cache_k, cache_v)


def _post_kernel(x_ref, om_ref, od_ref, os_ref, wg_ref, wb_ref, wo_ref, g_ref, b_ref, y_ref, *, alpha):
    x = x_ref[...]
    d = x.shape[1]
    gates = _sigmoid(_nn(x.astype(BF16), wg_ref[...]))
    merged = (gates[:, 0:d] * _nn(om_ref[...], wb_ref[0:MOBA_W, :])
              + gates[:, d:2 * d] * _nn(od_ref[...], wb_ref[MOBA_W:MOBA_W + DIFF_W, :])
              + gates[:, 2 * d:3 * d] * _nn(os_ref[...], wb_ref[MOBA_W + DIFF_W:, :]))
    mix = _nn(merged.astype(BF16), wo_ref[...])
    y_ref[...] = _layer_norm(alpha * x + mix, g_ref[...], b_ref[...])


def _post(x2d, o_m, o_d, o_s, w_gate, w_branch, w_out, ln_g, ln_b, alpha, tm):
    m, d = x2d.shape
    row = lambda i: (i, 0)
    full = lambda a: pl.BlockSpec(a.shape, lambda i: (0, 0))
    return pl.pallas_call(
        functools.partial(_post_kernel, alpha=alpha),
        grid=(m // tm,),
        in_specs=[pl.BlockSpec((tm, d), row), pl.BlockSpec((tm, MOBA_W), row),
                  pl.BlockSpec((tm, DIFF_W), row), pl.BlockSpec((tm, SB_W), row),
                  full(w_gate), full(w_branch), full(w_out), full(ln_g), full(ln_b)],
        out_specs=pl.BlockSpec((tm, d), row),
        out_shape=jax.ShapeDtypeStruct((m, d), F32),
        compiler_params=_params("arbitrary"), name="post",
    )(x2d, o_m, o_d, o_s, w_gate, w_branch, w_out, ln_g, ln_b)


def _gelu(x):
    return 0.5 * x * (1.0 + lax.erf(x * (2.0 ** -0.5)))


FFN_CHUNKS = 2


def _ffn_chunks(dff):
    tiles = dff // LANES
    assert tiles * LANES == dff
    bounds = [LANES * ((tiles * i) // FFN_CHUNKS) for i in range(FFN_CHUNKS + 1)]
    return [(bounds[i], bounds[i + 1]) for i in range(FFN_CHUNKS) if bounds[i + 1] > bounds[i]]


def _ffn_kernel(*refs, alpha, tm, seq, has_past):
    if has_past:
        x_ref, ple_ref, p1_ref, p2_ref = refs[:4]
        refs = refs[4:]
    else:
        x_ref, ple_ref = refs[:2]
        refs = refs[2:]
    (wup_ref, cw_ref, cb_ref, wdn_ref, g_ref, b_ref, wple_ref, wpg_ref,
     y_ref, conv_ref, aext_ref) = refs
    dff = cw_ref.shape[1]
    ti = pl.program_id(1)
    x = x_ref[...]
    xb = x.astype(BF16)

    @pl.when(ti == 0)
    def _():
        aext_ref[0:SUBLANES, :] = jnp.zeros((SUBLANES, dff), F32)

    ffn = jnp.zeros(x.shape, F32)
    for c0, c1 in _ffn_chunks(dff):
        cs = slice(c0, c1)
        a = _nn(xb, wup_ref[:, c0:c1])
        gate_in = _nn(xb, wup_ref[:, dff + c0:dff + c1])
        aext_ref[SUBLANES:SUBLANES + tm, cs] = a
        a1 = aext_ref[SUBLANES - 1:SUBLANES - 1 + tm, cs]
        a2 = aext_ref[SUBLANES - 2:SUBLANES - 2 + tm, cs]
        if has_past:
            t = _iota((tm, 1), 0) % seq
            a1 = jnp.where(t >= 1, a1, p1_ref[:, cs])
            a2 = jnp.where(t >= 2, a2, p2_ref[:, cs])
            conv_ref[:, cs] = a
        else:
            aext_ref[0:SUBLANES, cs] = a[tm - SUBLANES:, :]
            conv_ref[:, cs] = a[tm - SUBLANES:, :]
        a_conv = cb_ref[:, cs] + (cw_ref[0:1, cs] * a2 + cw_ref[1:2, cs] * a1 + cw_ref[2:3, cs] * a)
        hid = _gelu(a_conv) * gate_in
        ffn = ffn + _nn(hid.astype(BF16), wdn_ref[c0:c1, :])
    y = _layer_norm(alpha * x + ffn, g_ref[...], b_ref[...])
    pg = _sigmoid(_nn(y.astype(BF16), wpg_ref[...]))
    y_ref[...] = y + pg * _nn(ple_ref[...].astype(BF16), wple_ref[...])


def _ffn(x2d, ple2d, past, w_up, conv_w, conv_b, w_down, ln_g, ln_b, w_ple, w_pg, alpha, tm, seq):
    m, d = x2d.shape
    dff = conv_w.shape[1]
    has_past = past is not None
    if has_past:
        assert m == tm and tm % seq == 0
        grid = (1, 1)
        nb = 1
    else:
        assert seq % tm == 0 and tm >= SUBLANES
        nb = seq // tm
        grid = (m // seq, nb)
    row = lambda i, j: (i * nb + j, 0)
    full = lambda a: pl.BlockSpec(a.shape, lambda i, j: (0,) * a.ndim)
    in_specs = [pl.BlockSpec((tm, d), row), pl.BlockSpec((tm, ple2d.shape[1]), row)]
    args = [x2d, ple2d]
    if has_past:
        in_specs += [pl.BlockSpec((tm, dff), row)] * 2
        args += list(past)
        conv_shape = jax.ShapeDtypeStruct((m, dff), F32)
        conv_spec = pl.BlockSpec((tm, dff), row)
    else:
        conv_shape = jax.ShapeDtypeStruct((m // seq, SUBLANES, dff), F32)
        conv_spec = pl.BlockSpec((None, SUBLANES, dff), lambda i, j: (i, 0, 0))
    weights = [w_up, conv_w, conv_b, w_down, ln_g, ln_b, w_ple, w_pg]
    in_specs += [full(w) for w in weights]
    return pl.pallas_call(
        functools.partial(_ffn_kernel, alpha=alpha, tm=tm, seq=seq, has_past=has_past),
        grid=grid, in_specs=in_specs,
        out_specs=[pl.BlockSpec((tm, d), row), conv_spec],
        out_shape=[jax.ShapeDtypeStruct((m, d), F32), conv_shape],
        scratch_shapes=[pltpu.VMEM((tm + SUBLANES, dff), F32)],
        compiler_params=_params("arbitrary", "arbitrary"), name="ffn",
    )(*args, *weights)


def kernel(x_prompt, x_sample, cache_k, cache_v, state_conv, page_table, p_prompt, p_sample, w_in, w_gate, w_branch, w_out, diff_lambda, diff_subln_g, ln1_g, ln1_b, w_up, conv_w, conv_b, w_down, ln2_g, ln2_b, w_ple, w_ple_gate):
    depth = w_in.shape[0]
    alpha = (2.0 * depth) ** 0.25
    bp, tp, d = x_prompt.shape
    bs, ts, _ = x_sample.shape
    dff = conv_w.shape[2]
    tm_p = 512 if tp % 512 == 0 else MOBA_BLOCK
    tm_f = 256
    assert tp % MOBA_BLOCK == 0
    yp = x_prompt.reshape(bp * tp, d)
    ys = x_sample.reshape(bs * ts, d)
    outs = {k: [] for k in ("cp", "cs")}
    kv_p, kv_s = (), ()
    row2 = lambda a: a.reshape(1, -1)
    for i in range(depth):
        lam_init = 0.8 - 0.6 * math.exp(-0.3 * i)
        wi, wg, wbr, wo = (w[i].astype(BF16) for w in (w_in, w_gate, w_branch, w_out))
        wu, wd, wpl, wpg = (w[i].astype(BF16) for w in (w_up, w_down, w_ple, w_ple_gate))
        lam_p, subg = diff_lambda[i], row2(diff_subln_g[i])
        l1g, l1b, l2g, l2b, cb = (row2(a[i]) for a in (ln1_g, ln1_b, ln2_g, ln2_b, conv_b))
        cw = conv_w[i]

        *kv_p, qb, kb, vb, qmf, kmean = _proj(yp, wi, tm_p, True, i, depth, tuple(kv_p))
        to3 = lambda a: a.reshape(bp, tp, a.shape[-1])
        qb3, kb3, vb3 = to3(qb), to3(kb), to3(vb)
        o_m = _moba(qb3, to3(qmf), kmean.reshape(bp, tp // MOBA_BLOCK, MOBA_W), kb3, vb3)
        o_d = _diff(qb3, kb3, vb3, lam_p, subg.reshape(-1, 1), lam_init)
        o_s = _sb(qb3, kb3, vb3)
        x1 = _post(yp, o_m.reshape(-1, MOBA_W), o_d.reshape(-1, DIFF_W), o_s.reshape(-1, SB_W),
                   wg, wbr, wo, l1g, l1b, alpha, tm_f)
        yp, conv = _ffn(x1, p_prompt[i].reshape(bp * tp, -1), None, wu, cw, cb, wd, l2g, l2b, wpl, wpg,
                        alpha, tm_f, tp)
        outs["cp"].append(conv[:, SUBLANES - (CONV_W - 1):, :])

        ms = bs * ts
        *kv_s, qb, kb, vb, qmf = _proj(ys, wi, ms, False, i, depth, tuple(kv_s))
        to3 = lambda a: a.reshape(bs, ts, a.shape[-1])
        o = _sample_attn(i, to3(qb), to3(qmf), to3(kb), to3(vb), cache_k, cache_v, page_table,
                         lam_p, subg, lam_init).reshape(ms, -1).astype(BF16)
        x1 = _post(ys, o[:, :MOBA_W], o[:, MOBA_W:MOBA_W + DIFF_W], o[:, MOBA_W + DIFF_W:],
                   wg, wbr, wo, l1g, l1b, alpha, ms)
        st = state_conv[i]
        zeros = jnp.zeros((bs, ts - 1, dff), F32)
        p1 = jnp.concatenate([st[:, 1:2], zeros], axis=1).reshape(ms, dff)
        p2 = jnp.concatenate([st, zeros[:, 1:]], axis=1).reshape(ms, dff)
        ys, a_full = _ffn(x1, p_sample[i].reshape(ms, -1), (p1, p2), wu, cw, cb, wd, l2g, l2b, wpl, wpg,
                          alpha, ms, ts)
        outs["cs"].append(a_full.reshape(bs, ts, dff)[:, ts - (CONV_W - 1):, :])
    st = lambda name: jnp.stack(outs[name])
    k_p, v_p = (a.reshape(depth, bp, tp, -1) for a in kv_p)
    k_s, v_s = (a.reshape(depth, bs, ts, -1) for a in kv_s)
    return (yp.reshape(bp, tp, d), ys.reshape(bs, ts, d), k_p, v_p, st("cp"), k_s, v_s, st("cs"))
```

```python
import functools
import math

import numpy as np
import jax
import jax.numpy as jnp
from jax import lax
from jax.experimental import pallas as pl
from jax.experimental.pallas import tpu as pltpu

F32 = jnp.float32
BF16 = jnp.bfloat16

HEAD_DIM = 64
MOBA_HEADS = 4
MOBA_BLOCK = 256
MOBA_TOPK = 3
DIFF_HEADS = 4
SB_HEADS = 4
MOBA_W = MOBA_HEADS * HEAD_DIM
DIFF_W = DIFF_HEADS * 2 * HEAD_DIM
SB_W = SB_HEADS * HEAD_DIM
QK_W = MOBA_W + DIFF_W + SB_W
N_GROUPS = QK_W // HEAD_DIM
N_BRANCH = 3
N_ALIBI = MOBA_HEADS + DIFF_HEADS
CONV_W = 3
LN_EPS = 1e-5
QK_SCALE = HEAD_DIM ** -0.5
NEG = -1e30
LANES = 128
SUBLANES = 8
VMEM_LIMIT = 56 * 1024 * 1024
TQ = MOBA_BLOCK
GATE_ROWS = 2 * SUBLANES
DECODE_BLOCKS_PER_STEP = 4
DECODE_SLOTS = 3
SEQS_PER_STEP = 2
SB_DEAD_LOG2 = 160.0
DIFF_HEADS_PER_CHAIN = 2
ONES_ROWS = 2 * SUBLANES
LOG2E = math.log2(math.e)

_SLOPES = [2.0 ** (-8.0 * h / N_ALIBI) for h in range(1, N_ALIBI + 1)]
MOBA_SLOPES = _SLOPES[0::2]
DIFF_SLOPES = _SLOPES[1::2]


def _nt(a, b):
    return lax.dot_general(a, b, (((1,), (1,)), ((), ())), preferred_element_type=F32)


def _nn(a, b):
    return jnp.dot(a, b, preferred_element_type=F32)


def _split(x):
    hi = x.astype(BF16)
    lo = (x - hi.astype(F32)).astype(BF16)
    return hi, lo


def _sigmoid(x):
    return 1.0 / (1.0 + jnp.exp(-x))


def _softplus(z):
    return jnp.maximum(z, 0.0) + jnp.log(1.0 + jnp.exp(-jnp.abs(z)))


def _layer_norm(x, g, b):
    mu = jnp.mean(x, axis=-1, keepdims=True)
    xc = x - mu
    var = jnp.mean(xc * xc, axis=-1, keepdims=True)
    return xc * lax.rsqrt(var + LN_EPS) * g + b


def _params(*sem):
    return pltpu.CompilerParams(dimension_semantics=sem, vmem_limit_bytes=VMEM_LIMIT)


def _iota(shape, dim):
    return lax.broadcasted_iota(jnp.int32, shape, dim)


def _proj_kernel(*refs, n_blk, n_carried):
    x_ref, w_ref = refs[:2]
    k_ref, v_ref, qb_ref, kb_ref, vb_ref, qmf_ref, *rest = refs[2 + n_carried:]
    xb = x_ref[...].astype(BF16)
    q = _nn(xb, w_ref[:, 0:QK_W])
    k = _nn(xb, w_ref[:, QK_W:2 * QK_W])
    v = _nn(xb, w_ref[:, 2 * QK_W:3 * QK_W])
    if n_carried:
        k_ref[...] = k
        v_ref[...] = v
    else:
        k_ref[0] = k
        v_ref[0] = v
        for later in range(1, k_ref.shape[0]):
            k_ref[later] = jnp.zeros_like(k)
            v_ref[later] = jnp.zeros_like(v)
    qb_ref[...] = (q * QK_SCALE).astype(BF16)
    kb_ref[...] = k.astype(BF16)
    vb_ref[...] = v.astype(BF16)
    qmf_ref[...] = q[:, :MOBA_W]
    if n_blk:
        kmean_ref = rest[0]
        for j in range(n_blk):
            blk = k[j * MOBA_BLOCK:(j + 1) * MOBA_BLOCK, :MOBA_W]
            kmean_ref[j:j + 1, :] = jnp.sum(blk, axis=0, keepdims=True) * (1.0 / MOBA_BLOCK)


def _proj(x2d, w, tm, with_kmean, layer, depth, kv_all=()):
    m, d = x2d.shape
    n_blk = tm // MOBA_BLOCK if with_kmean else 0
    row = lambda i: (i, 0)
    out_shape = [jax.ShapeDtypeStruct((depth, m, QK_W), F32), jax.ShapeDtypeStruct((depth, m, QK_W), F32),
                 jax.ShapeDtypeStruct((m, QK_W), BF16), jax.ShapeDtypeStruct((m, QK_W), BF16),
                 jax.ShapeDtypeStruct((m, QK_W), BF16), jax.ShapeDtypeStruct((m, MOBA_W), F32)]
    assert bool(kv_all) == (layer > 0)
    kv_spec = (pl.BlockSpec((None, tm, QK_W), lambda i: (layer, i, 0)) if kv_all
               else pl.BlockSpec((depth, tm, QK_W), lambda i: (0, i, 0)))
    out_specs = ([kv_spec] * 2
                 + [pl.BlockSpec((tm, QK_W), row)] * 3 + [pl.BlockSpec((tm, MOBA_W), row)])
    if with_kmean:
        out_shape.append(jax.ShapeDtypeStruct((m // tm, n_blk, MOBA_W), F32))
        out_specs.append(pl.BlockSpec((None, n_blk, MOBA_W), lambda i: (i, 0, 0)))
    in_specs = [pl.BlockSpec((tm, d), row), pl.BlockSpec(w.shape, lambda i: (0, 0))]
    in_specs += [pl.BlockSpec(memory_space=pl.ANY)] * len(kv_all)
    return pl.pallas_call(
        functools.partial(_proj_kernel, n_blk=n_blk, n_carried=len(kv_all)),
        grid=(m // tm,),
        in_specs=in_specs, out_specs=out_specs, out_shape=out_shape,
        input_output_aliases={2 + i: i for i in range(len(kv_all))},
        compiler_params=_params("arbitrary"), name="proj",
    )(x2d, w, *kv_all)


def _fill_vt(v_ref, vt_ref, n_heads, width):
    t = v_ref.shape[0]
    extra = vt_ref.shape[1] - width
    for h in range(n_heads):
        if extra:
            vt_ref[h, width:, :] = jnp.ones((extra, t), BF16)
    for j in range(t // TQ):
        blk = v_ref[j * TQ:(j + 1) * TQ, :].astype(F32).T.astype(BF16)
        for h in range(n_heads):
            vt_ref[h, :width, j * TQ:(j + 1) * TQ] = blk[h * width:(h + 1) * width, :]


def _stack_heads_t(q_bf16, n_heads, width):
    qt = q_bf16.astype(F32).T * LOG2E
    row_head = _iota((n_heads * width, 1), 0) // width
    return jnp.concatenate([jnp.where(row_head == h, qt, 0.0).astype(BF16) for h in range(n_heads)], axis=1)


def _lane_slopes(slopes, n_lanes):
    lane_head = _iota((1, n_lanes), 1) // TQ
    out = jnp.zeros((1, n_lanes), F32)
    for h, s in enumerate(slopes):
        out = jnp.where(lane_head == h, s, out)
    return out


def _softmax_step(s, m, extra):
    cand = jnp.max(s, axis=0, keepdims=True) + extra
    m_new = jnp.maximum(m, cand)
    alpha = jnp.exp2(m - m_new)
    p = jnp.exp2(s - (m_new - extra))
    return p.astype(BF16), alpha, m_new


def _moba_kernel(q_ref, qf_ref, kmean_ref, k_ref, v_ref, o_ref, vt_ref, bias_ref, selb_ref, acc_ref, *, n_blk):
    nq = MOBA_HEADS * TQ
    qi = pl.program_id(1)

    slope_row = _lane_slopes(MOBA_SLOPES, nq) * LOG2E

    @pl.when(qi == 0)
    def _():
        _fill_vt(v_ref, vt_ref, MOBA_HEADS, HEAD_DIM)
        bias_ref[...] = _iota((TQ, nq), 0).astype(F32) * slope_row

    qt_all = _stack_heads_t(q_ref[...], MOBA_HEADS, HEAD_DIM)

    qft_hi, qft_lo = _split(qf_ref[...].T)
    km = kmean_ref[...]
    km = jnp.concatenate([km, jnp.zeros((GATE_ROWS - n_blk, MOBA_W), F32)], axis=0)
    lane_head = _iota((1, MOBA_W), 1) // HEAD_DIM
    blk_row = _iota((GATE_ROWS, 1), 0)
    for h in range(MOBA_HEADS):
        kh_hi, kh_lo = _split(jnp.where(lane_head == h, km, 0.0))
        g = _nn(kh_hi, qft_hi) + _nn(kh_hi, qft_lo) + _nn(kh_lo, qft_hi)
        cnt = jnp.zeros((GATE_ROWS, TQ), F32)
        for mb in range(n_blk - 1):
            gm = g[mb:mb + 1, :]
            tie = jnp.where(blk_row > mb, 1.0, 0.0)
            beats = jnp.where(gm > g, 1.0, jnp.where(gm == g, tie, 0.0))
            cnt = cnt + beats * jnp.where(mb < qi, 1.0, 0.0)
        sel = jnp.where(blk_row < qi, jnp.where(cnt < MOBA_TOPK, 0.0, NEG), NEG)
        selb_ref[:, h * TQ:(h + 1) * TQ] = sel

    acc_ref[...] = jnp.zeros_like(acc_ref)

    def scores(n, diag):
        s = _nn(k_ref[pl.ds(pl.multiple_of(n * TQ, TQ), TQ), :], qt_all) + bias_ref[...]
        if diag:
            s = jnp.where(_iota((TQ, nq), 0) <= _iota((TQ, nq), 1) % TQ, s, NEG)
        return s

    def attend(s, n, diag, m):
        start = pl.multiple_of(n * TQ, TQ)
        extra = slope_row * ((n - qi) * TQ).astype(F32)
        if not diag:
            extra = extra + selb_ref[pl.ds(n, 1), :]
        pb, alpha, m = _softmax_step(s, m, extra)
        for h in range(MOBA_HEADS):
            cols = slice(h * TQ, (h + 1) * TQ)
            pv = _nn(vt_ref[h, :, pl.ds(start, TQ)], pb[:, cols])
            acc_ref[h] = alpha[:, cols] * acc_ref[h] + pv
        return m

    m = attend(scores(qi, True), qi, True, jnp.full((1, nq), NEG, F32))
    lax.fori_loop(0, qi, lambda n, m: attend(scores(n, False), n, False, m), m)
    outs = []
    for h in range(MOBA_HEADS):
        acc = acc_ref[h]
        outs.append(acc[:HEAD_DIM] * (1.0 / acc[HEAD_DIM:HEAD_DIM + 1]))
    o_ref[...] = jnp.concatenate(outs, axis=0).T.astype(o_ref.dtype)


def _moba(qb, qmf, kmean, kb, vb):
    b, t, _ = qb.shape
    n_blk = t // TQ
    assert n_blk <= GATE_ROWS
    nq = MOBA_HEADS * TQ
    return pl.pallas_call(
        functools.partial(_moba_kernel, n_blk=n_blk),
        grid=(b, n_blk),
        in_specs=[pl.BlockSpec((None, TQ, MOBA_W), lambda i, j: (i, j, 0)),
                  pl.BlockSpec((None, TQ, MOBA_W), lambda i, j: (i, j, 0)),
                  pl.BlockSpec((None, n_blk, MOBA_W), lambda i, j: (i, 0, 0)),
                  pl.BlockSpec((None, t, MOBA_W), lambda i, j: (i, 0, 0)),
                  pl.BlockSpec((None, t, MOBA_W), lambda i, j: (i, 0, 0))],
        out_specs=pl.BlockSpec((None, TQ, MOBA_W), lambda i, j: (i, j, 0)),
        out_shape=jax.ShapeDtypeStruct((b, t, MOBA_W), BF16),
        scratch_shapes=[pltpu.VMEM((MOBA_HEADS, HEAD_DIM + ONES_ROWS, t), BF16), pltpu.VMEM((TQ, nq), F32),
                        pltpu.VMEM((GATE_ROWS, nq), F32),
                        pltpu.VMEM((MOBA_HEADS, HEAD_DIM + ONES_ROWS, TQ), F32)],
        compiler_params=_params("arbitrary", "arbitrary"), name="moba",
    )(qb, qmf, kmean, kb, vb)


def _suffix_matrix(n, transposed):
    r = _iota((n, n), 0)
    c = _iota((n, n), 1)
    return jnp.where((c > r) if transposed else (r > c), 1.0, 0.0).astype(BF16)


def _sb_kernel(q_ref, k_ref, v_ref, o_ref, vt_ref, acc_ref):
    nq = SB_HEADS * TQ
    nc = q_ref.shape[0]
    qi = pl.program_id(1)

    @pl.when(qi == 0)
    def _():
        for c in range(nc):
            _fill_vt(v_ref.at[c], vt_ref.at[c], SB_HEADS, HEAD_DIM)

    qt_all = [_stack_heads_t(q_ref[c], SB_HEADS, HEAD_DIM) for c in range(nc)]
    r = _iota((TQ + ONES_ROWS, 2 * TQ), 0)
    c = _iota((TQ + ONES_ROWS, 2 * TQ), 1) % TQ
    later_mat = jnp.where(r >= TQ, -1.0, jnp.where(c > r, -1.0, 0.0)).astype(BF16)
    acc_ref[...] = jnp.zeros_like(acc_ref)

    def scores(c, n):
        return _nn(k_ref[c, pl.ds(pl.multiple_of(n * TQ, TQ), TQ), :], qt_all[c])

    def attend(c, z, n, diag, cs):
        start = pl.multiple_of(n * TQ, TQ)
        sp = jnp.maximum(z, 0.0) + jnp.log(1.0 + jnp.exp2(-jnp.abs(z))) * LOG2E
        if diag:
            before = _iota((TQ, nq), 0) < _iota((TQ, nq), 1) % TQ
            sp_keep = jnp.where(before, sp, 0.0)
        else:
            sp_keep = sp
        sums = _nn(later_mat, jnp.concatenate(_split(sp_keep), axis=0))
        a = jnp.exp2((z - sp) + sums[:TQ] + cs)
        if diag:
            a = jnp.where(before, a, 0.0)
        ab = a.astype(BF16)
        for h in range(SB_HEADS):
            acc_ref[c, h] += _nn(vt_ref[c, h, :, pl.ds(start, TQ)], ab[:, h * TQ:(h + 1) * TQ])
        return cs + sums[TQ:TQ + 1]

    def step(n, diag, css):
        return tuple(attend(c, scores(c, n), n, diag, css[c]) for c in range(nc))

    css = step(qi, True, tuple(jnp.zeros((1, nq), F32) for _ in range(nc)))

    def alive(carry):
        i, css = carry
        top = jnp.max(css[0])
        for c in range(1, nc):
            top = jnp.maximum(top, jnp.max(css[c]))
        return jnp.logical_and(i < qi, top > -SB_DEAD_LOG2)

    lax.while_loop(alive, lambda carry: (carry[0] + 1, step(qi - 1 - carry[0], False, carry[1])),
                   (jnp.int32(0), css))
    for c in range(nc):
        o_ref[c] = jnp.concatenate([acc_ref[c, h] for h in range(SB_HEADS)], axis=0).T.astype(o_ref.dtype)


def _sb(qb, kb, vb):
    b, t, _ = qb.shape
    cb = (QK_W - SB_W) // SB_W
    nc = math.gcd(b, SEQS_PER_STEP)
    return pl.pallas_call(
        _sb_kernel,
        grid=(b // nc, t // TQ),
        in_specs=[pl.BlockSpec((nc, TQ, SB_W), lambda i, j: (i, j, cb)),
                  pl.BlockSpec((nc, t, SB_W), lambda i, j: (i, 0, cb)),
                  pl.BlockSpec((nc, t, SB_W), lambda i, j: (i, 0, cb))],
        out_specs=pl.BlockSpec((nc, TQ, SB_W), lambda i, j: (i, j, 0)),
        out_shape=jax.ShapeDtypeStruct((b, t, SB_W), BF16),
        scratch_shapes=[pltpu.VMEM((nc, SB_HEADS, HEAD_DIM, t), BF16),
                        pltpu.VMEM((nc, SB_HEADS, HEAD_DIM, TQ), F32)],
        compiler_params=_params("arbitrary", "arbitrary"), name="sb",
    )(qb, kb, vb)


def _diff_lambda(lam_ref, lam_init):
    dl = lam_ref[...]
    s1 = jnp.sum(dl[0:1] * dl[1:2], axis=1, keepdims=True)
    s2 = jnp.sum(dl[2:3] * dl[3:4], axis=1, keepdims=True)
    return jnp.exp(s1) - jnp.exp(s2) + lam_init


def _diff_kernel(*refs, lam_init):
    hd = 2 * HEAD_DIM
    nh = DIFF_HEADS_PER_CHAIN
    nc = DIFF_HEADS // nh
    nq = nh * 2 * TQ
    q_refs, k_refs, v_refs = refs[:nc], refs[nc:2 * nc], refs[2 * nc:3 * nc]
    lam_ref, g_ref, o_ref, vt_ref, bias_ref, acc_ref = refs[3 * nc:]
    qi = pl.program_id(1)
    slope_rows = []
    for c in range(nc):
        lane_head = _iota((1, nq), 1) // (2 * TQ) + c * nh
        row = jnp.zeros((1, nq), F32)
        for i in range(DIFF_HEADS):
            row = jnp.where(lane_head == i, DIFF_SLOPES[i] * LOG2E, row)
        slope_rows.append(row)

    @pl.when(qi == 0)
    def _():
        for c in range(nc):
            _fill_vt(v_refs[c], vt_ref.at[c], nh, hd)
            bias_ref[c] = _iota((TQ, nq), 0).astype(F32) * slope_rows[c]

    qt_all = [_stack_heads_t(q_refs[c][...], 2 * nh, HEAD_DIM) for c in range(nc)]
    acc_ref[...] = jnp.zeros_like(acc_ref)

    def scores(c, n, diag):
        s = _nn(k_refs[c][pl.ds(pl.multiple_of(n * TQ, TQ), TQ), :], qt_all[c]) + bias_ref[c]
        if diag:
            s = jnp.where(_iota((TQ, nq), 0) <= _iota((TQ, nq), 1) % TQ, s, NEG)
        return s

    def attend(c, s, n, m):
        start = pl.multiple_of(n * TQ, TQ)
        extra = slope_rows[c] * ((n - qi) * TQ).astype(F32)
        pb, alpha, m = _softmax_step(s, m, extra)
        for h in range(nh):
            cols = slice(h * 2 * TQ, (h + 1) * 2 * TQ)
            pv = _nn(vt_ref[c, h, :, pl.ds(start, TQ)], pb[:, cols])
            acc_ref[c, h] = alpha[:, cols] * acc_ref[c, h] + pv
        return m

    def step(n, diag, ms):
        return tuple(attend(c, scores(c, n, diag), n, ms[c]) for c in range(nc))

    ms = step(qi, True, tuple(jnp.full((1, nq), NEG, F32) for _ in range(nc)))
    lax.fori_loop(0, qi, lambda n, ms: step(n, False, ms), ms)
    lam = _diff_lambda(lam_ref, lam_init)
    outs = []
    for c in range(nc):
        for h in range(nh):
            acc = acc_ref[c, h]
            acc = acc[:hd] * (1.0 / acc[hd:hd + 1])
            o = acc[:, :TQ] - lam * acc[:, TQ:]
            ms2 = jnp.mean(o * o, axis=0, keepdims=True)
            outs.append(o * lax.rsqrt(ms2 + LN_EPS) * g_ref[...] * (1.0 - lam_init))
    o_ref[...] = jnp.concatenate(outs, axis=0).T.astype(o_ref.dtype)


def _diff(qb, kb, vb, lam_p, subln_g_col, lam_init):
    b, t, _ = qb.shape
    nh = DIFF_HEADS_PER_CHAIN
    nc = DIFF_HEADS // nh
    w = nh * 2 * HEAD_DIM
    assert MOBA_W % w == 0 and DIFF_HEADS % nh == 0
    c0 = MOBA_W // w
    tile = lambda c: pl.BlockSpec((None, TQ, w), lambda i, j: (i, j, c0 + c))
    seq = lambda c: pl.BlockSpec((None, t, w), lambda i, j: (i, 0, c0 + c))
    rows = 2 * HEAD_DIM + ONES_ROWS
    return pl.pallas_call(
        functools.partial(_diff_kernel, lam_init=lam_init),
        grid=(b, t // TQ),
        in_specs=([tile(c) for c in range(nc)] + [seq(c) for c in range(nc)] * 2
                  + [pl.BlockSpec(lam_p.shape, lambda i, j: (0, 0)),
                     pl.BlockSpec(subln_g_col.shape, lambda i, j: (0, 0))]),
        out_specs=pl.BlockSpec((None, TQ, DIFF_W), lambda i, j: (i, j, 0)),
        out_shape=jax.ShapeDtypeStruct((b, t, DIFF_W), BF16),
        scratch_shapes=[pltpu.VMEM((nc, nh, rows, t), BF16),
                        pltpu.VMEM((nc, TQ, nh * 2 * TQ), F32),
                        pltpu.VMEM((nc, nh, rows, 2 * TQ), F32)],
        compiler_params=_params("arbitrary", "arbitrary"), name="diff",
    )(*([qb] * nc), *([kb] * nc), *([vb] * nc), lam_p, subln_g_col)


def _sample_attn_kernel(pt_ref, qbd_ref, qg_ref, kn_ref, vn_ref, lam_ref, g_ref, ck_ref, cv_ref, o_ref,
                        acc_ref, m_ref, l_ref, cs_ref, gall_ref, mall_ref, lall_ref, oall_ref,
                        kbuf_ref, vbuf_ref, sem_ref, *,
                        layer, n_blk, tq, past_len, lam_init, n_pages_step, blocks_per_step):
    rows = tq * N_GROUPS
    b = pl.program_id(0)
    j = pl.program_id(1)
    last = pl.num_programs(1) - 1
    page = kbuf_ref.shape[2]
    ppb = MOBA_BLOCK // page
    n_chunks = n_blk // blocks_per_step
    total_chunks = pl.num_programs(0) * n_chunks

    def chunk_copies(c):
        slot = c % DECODE_SLOTS
        seq = c // n_chunks
        first_page = (n_blk - (c % n_chunks + 1) * blocks_per_step) * ppb
        copies = []
        for p in range(n_pages_step):
            pid = pt_ref[seq, first_page + p]
            copies.append(pltpu.make_async_copy(ck_ref.at[layer, pid], kbuf_ref.at[slot, p],
                                                sem_ref.at[slot, p]))
            copies.append(pltpu.make_async_copy(cv_ref.at[layer, pid], vbuf_ref.at[slot, p],
                                                sem_ref.at[slot, n_pages_step + p]))
        return copies

    def start_chunk(c):
        @pl.when(c < total_chunks)
        def _():
            for cp in chunk_copies(c):
                cp.start()

    @pl.when((b == 0) & (j == 0))
    def _():
        for c in range(DECODE_SLOTS - 1):
            start_chunk(c)

    row = _iota((rows, 1), 0)
    grp = row % N_GROUPS
    qidx = row // N_GROUPS
    kind = jnp.where(grp < MOBA_HEADS, 0, jnp.where(grp < MOBA_HEADS + 2 * DIFF_HEADS, 1, 2))
    is_moba = kind == 0
    is_diff = kind == 1
    is_sb = kind == 2
    slope = jnp.zeros((rows, 1), F32)
    for h in range(MOBA_HEADS):
        slope = jnp.where(grp == h, MOBA_SLOPES[h], slope)
    for h in range(DIFF_HEADS):
        slope = jnp.where(grp - MOBA_HEADS - 2 * h == 0, DIFF_SLOPES[h], slope)
        slope = jnp.where(grp - MOBA_HEADS - 2 * h == 1, DIFF_SLOPES[h], slope)
    blk_lane = _iota((rows, LANES), 1)

    @pl.when(j == 0)
    def _():
        acc_ref[...] = jnp.zeros_like(acc_ref)
        m_ref[...] = jnp.full_like(m_ref, NEG)
        l_ref[...] = jnp.zeros_like(l_ref)
        cs_ref[...] = jnp.zeros_like(cs_ref)
        gall_ref[...] = jnp.zeros_like(gall_ref)
        mall_ref[...] = jnp.full_like(mall_ref, NEG)
        lall_ref[...] = jnp.zeros_like(lall_ref)

    def process(kb, vb, seg, n_first, pos0, limit):
        nk = kb.shape[0]
        n_seg = nk // seg
        s_raw = _nt(qbd_ref[...], kb)
        col = _iota((1, nk), 1)
        s = s_raw + slope * (col.astype(F32) + pos0)
        if limit is not None:
            valid = col < limit
            s = jnp.where(valid, s, NEG)
        segs = [slice(i * seg, (i + 1) * seg) for i in range(n_seg)]
        seg_max = [jnp.max(s[:, sl], axis=1, keepdims=True) for sl in segs]
        m_old = m_ref[...]
        m_new = m_old
        for sm in seg_max:
            m_new = jnp.maximum(m_new, sm)
        alpha = jnp.exp(m_old - m_new)
        sp = _softplus(s_raw)
        lk = -sp
        if limit is not None:
            lk = jnp.where(valid, lk, 0.0)
        later_mat = _suffix_matrix(seg, False)
        run = cs_ref[...]
        a_segs = [None] * n_seg
        for i in reversed(range(n_seg)):
            lk_hi, lk_lo = _split(lk[:, segs[i]])
            later = _nn(lk_hi, later_mat) + _nn(lk_lo, later_mat) + run
            a_segs[i] = jnp.exp((s_raw[:, segs[i]] - sp[:, segs[i]]) + later)
            run = run + jnp.sum(lk[:, segs[i]], axis=1, keepdims=True)
        cs_ref[...] = run
        psum = jnp.zeros((rows, 1), F32)
        p_segs = []
        mall, lall = mall_ref[...], lall_ref[...]
        for i in range(n_seg):
            p = jnp.exp(s[:, segs[i]] - jnp.where(is_moba, seg_max[i], m_new))
            ps = jnp.sum(p, axis=1, keepdims=True)
            psum = psum + ps
            mall = jnp.where(blk_lane == n_first + i, seg_max[i], mall)
            lall = jnp.where(blk_lane == n_first + i, ps, lall)
            a = a_segs[i]
            if limit is not None:
                a = jnp.where(valid[:, segs[i]], a, 0.0)
            p_segs.append(jnp.where(is_sb, a, p).astype(BF16))
        mall_ref[...] = mall
        lall_ref[...] = lall
        l_ref[...] = jnp.where(is_diff, alpha * l_ref[...] + psum, l_ref[...])
        m_ref[...] = jnp.where(is_diff, m_new, m_old)
        for i in range(n_seg):
            oall_ref[n_first + i] = _nn(p_segs[i], vb[segs[i], :MOBA_W])
        pmat = p_segs[0] if n_seg == 1 else jnp.concatenate(p_segs, axis=1)
        fac = jnp.where(is_diff, alpha, jnp.where(is_sb, 1.0, 0.0))
        acc_ref[...] = acc_ref[...] * fac + _nn(pmat, vb[:, MOBA_W:])

    @pl.when(j == 0)
    def _():
        limit = jnp.where(is_sb, qidx, qidx + 1)
        process(kn_ref[...], vn_ref[...], kn_ref.shape[0], n_blk, 0.0, limit)

    @pl.when(j > 0)
    def _():
        chunk = b * n_chunks + j - 1
        for cp in chunk_copies(chunk):
            cp.wait()
        start_chunk(chunk + DECODE_SLOTS - 1)
        slot = chunk % DECODE_SLOTS
        n_first = n_blk - j * blocks_per_step
        gall = gall_ref[...]
        for i in range(blocks_per_step):
            ksum = jnp.zeros((1, MOBA_W), F32)
            for p in range(ppb):
                ksum = ksum + jnp.sum(kbuf_ref[slot, i * ppb + p, :, :MOBA_W], axis=0, keepdims=True)
            gate = jnp.sum(qg_ref[...] * (ksum * (1.0 / MOBA_BLOCK)), axis=1, keepdims=True)
            gall = jnp.where(blk_lane == n_first + i, gate, gall)
        gall_ref[...] = gall
        kb = jnp.concatenate([kbuf_ref[slot, p].astype(BF16) for p in range(n_pages_step)], axis=0)
        vb = jnp.concatenate([vbuf_ref[slot, p].astype(BF16) for p in range(n_pages_step)], axis=0)
        pos0 = (n_first * MOBA_BLOCK - past_len).astype(F32)
        process(kb, vb, MOBA_BLOCK, n_first, pos0, None)

    @pl.when(j == last)
    def _():
        g = gall_ref[...]
        cnt = jnp.zeros((rows, LANES), F32)
        for mb in range(n_blk):
            gm = g[:, mb:mb + 1]
            tie = jnp.where(blk_lane > mb, 1.0, 0.0)
            cnt = cnt + jnp.where(gm > g, 1.0, jnp.where(gm == g, tie, 0.0))
        rank_lim = jnp.where(blk_lane < n_blk, float(MOBA_TOPK), jnp.where(blk_lane == n_blk, 1e9, -1.0))
        sel = cnt < rank_lim
        mall = jnp.where(sel, mall_ref[...], NEG)
        m_tot = jnp.max(mall, axis=1, keepdims=True)
        w = jnp.where(sel, jnp.exp(mall - m_tot), 0.0)
        l_tot = jnp.sum(w * lall_ref[...], axis=1, keepdims=True)
        o_m = jnp.zeros((rows, MOBA_W), F32)
        for nb in range(n_blk + 1):
            o_m = o_m + w[:, nb:nb + 1] * oall_ref[nb]
        o_m = o_m * jnp.where(is_moba, 1.0 / l_tot, 0.0)
        rest = acc_ref[...] * jnp.where(is_diff, 1.0 / l_ref[...], jnp.where(is_sb, 1.0, 0.0))
        acc = jnp.concatenate([o_m, rest], axis=1)
        lam = _diff_lambda(lam_ref, lam_init)
        g16 = _iota((N_GROUPS, 1), 0)
        cgrp = _iota((1, QK_W), 1) // HEAD_DIM
        d_lo, d_hi = MOBA_HEADS, MOBA_HEADS + 2 * DIFF_HEADS
        g_diff = jnp.where(g16 >= d_lo, jnp.where(g16 < d_hi, 1, 0), 0)
        c_diff = jnp.where(cgrp >= d_lo, jnp.where(cgrp < d_hi, 1, 0), 0)
        g_key = jnp.where(g_diff == 1, d_lo + 2 * ((g16 - d_lo) // 2), g16)
        c_key = jnp.where(c_diff == 1, d_lo + 2 * ((cgrp - d_lo) // 2), cgrp)
        second_half = g_diff * ((g16 - d_lo) % 2)
        coef = jnp.where(g_key == c_key, jnp.where(second_half == 1, -lam, 1.0), 0.0)
        hd = 2 * HEAD_DIM
        gsub = g_ref[...]
        for t in range(tq):
            o_t = jnp.sum(acc[t * N_GROUPS:(t + 1) * N_GROUPS, :] * coef, axis=0, keepdims=True)
            pieces = [o_t[:, :MOBA_W]]
            for h in range(DIFF_HEADS):
                od = o_t[:, MOBA_W + h * hd:MOBA_W + (h + 1) * hd]
                ms2 = jnp.mean(od * od, axis=1, keepdims=True)
                pieces.append(od * lax.rsqrt(ms2 + LN_EPS) * gsub * (1.0 - lam_init))
            pieces.append(o_t[:, MOBA_W + DIFF_W:])
            o_ref[t:t + 1, :] = jnp.concatenate(pieces, axis=1)


def _sample_attn(layer, qb, qmf, kb_new, vb_new, cache_k, cache_v, page_table, lam_p, subln_g, lam_init):
    b, tq, _ = qb.shape
    page = cache_k.shape[2]
    n_pages = page_table.shape[1]
    past_len = n_pages * page
    assert past_len % MOBA_BLOCK == 0 and MOBA_BLOCK % page == 0 and tq <= SUBLANES
    n_blk = past_len // MOBA_BLOCK
    assert n_blk + 1 <= LANES
    bps = math.gcd(n_blk, DECODE_BLOCKS_PER_STEP)
    pps = bps * (MOBA_BLOCK // page)
    rows = tq * N_GROUPS
    n_new = LANES

    gmask = (np.arange(QK_W)[None, :] // HEAD_DIM == np.arange(N_GROUPS)[:, None])
    qbd = jnp.where(gmask[None, None], qb[:, :, None, :], jnp.zeros((), BF16)).reshape(b, rows, QK_W)
    qg = jnp.where(gmask[None, None, :, :MOBA_W], qmf[:, :, None, :], 0.0).reshape(b, rows, MOBA_W)
    pad = ((0, 0), (0, n_new - tq), (0, 0))
    kn = jnp.pad(kb_new, pad)
    vn = jnp.pad(vb_new, pad)

    per_b = lambda i, j, pt: (i, 0, 0)
    whole = lambda i, j, pt: (0, 0)
    in_specs = [pl.BlockSpec((None, rows, QK_W), per_b),
                pl.BlockSpec((None, rows, MOBA_W), per_b),
                pl.BlockSpec((None, n_new, QK_W), per_b),
                pl.BlockSpec((None, n_new, QK_W), per_b),
                pl.BlockSpec(lam_p.shape, whole),
                pl.BlockSpec(subln_g.shape, whole),
                pl.BlockSpec(memory_space=pl.ANY),
                pl.BlockSpec(memory_space=pl.ANY)]
    grid_spec = pltpu.PrefetchScalarGridSpec(
        num_scalar_prefetch=1, grid=(b, n_blk // bps + 1), in_specs=in_specs,
        out_specs=pl.BlockSpec((None, tq, QK_W), per_b),
        scratch_shapes=[pltpu.VMEM((rows, QK_W - MOBA_W), F32), pltpu.VMEM((rows, 1), F32),
                        pltpu.VMEM((rows, 1), F32), pltpu.VMEM((rows, 1), F32),
                        pltpu.VMEM((rows, LANES), F32), pltpu.VMEM((rows, LANES), F32),
                        pltpu.VMEM((rows, LANES), F32), pltpu.VMEM((n_blk + 1, rows, MOBA_W), F32),
                        pltpu.VMEM((DECODE_SLOTS, pps, page, QK_W), F32),
                        pltpu.VMEM((DECODE_SLOTS, pps, page, QK_W), F32),
                        pltpu.SemaphoreType.DMA((DECODE_SLOTS, 2 * pps))])
    return pl.pallas_call(
        functools.partial(_sample_attn_kernel, layer=layer, n_blk=n_blk, tq=tq, past_len=past_len,
                          lam_init=lam_init, n_pages_step=pps, blocks_per_step=bps),
        grid_spec=grid_spec,
        out_shape=jax.ShapeDtypeStruct((b, tq, QK_W), F32),
        compiler_params=_params("arbitrary", "arbitrary"), name="sample_attn",
    )(page_table, qbd, qg, kn, vn, lam_p, subln_g, cache_k, cache_v)


def _row_chains(tm):
    half = tm // 2
    if half % MOBA_BLOCK:
        return [slice(0, tm)]
    return [slice(0, half), slice(half, tm)]


def _post_kernel(x_ref, om_ref, od_ref, os_ref, wg_ref, wb_ref, wo_ref, g_ref, b_ref, y_ref, *, alpha):
    tm, d = x_ref.shape
    bounds = (0, MOBA_W, MOBA_W + DIFF_W, MOBA_W + DIFF_W + SB_W)
    for rows in _row_chains(tm):
        x = x_ref[rows, :]
        xb = x.astype(BF16)
        merged = jnp.zeros(x.shape, F32)
        for i, o_ref in enumerate((om_ref, od_ref, os_ref)):
            gate = _sigmoid(_nn(xb, wg_ref[:, i * d:(i + 1) * d]))
            merged = merged + gate * _nn(o_ref[rows, :], wb_ref[bounds[i]:bounds[i + 1], :])
        mix = _nn(merged.astype(BF16), wo_ref[...])
        y_ref[rows, :] = _layer_norm(alpha * x + mix, g_ref[...], b_ref[...])


def _post(x2d, o_m, o_d, o_s, w_gate, w_branch, w_out, ln_g, ln_b, alpha, tm):
    m, d = x2d.shape
    row = lambda i: (i, 0)
    full = lambda a: pl.BlockSpec(a.shape, lambda i: (0, 0))
    return pl.pallas_call(
        functools.partial(_post_kernel, alpha=alpha),
        grid=(m // tm,),
        in_specs=[pl.BlockSpec((tm, d), row), pl.BlockSpec((tm, MOBA_W), row),
                  pl.BlockSpec((tm, DIFF_W), row), pl.BlockSpec((tm, SB_W), row),
                  full(w_gate), full(w_branch), full(w_out), full(ln_g), full(ln_b)],
        out_specs=pl.BlockSpec((tm, d), row),
        out_shape=jax.ShapeDtypeStruct((m, d), F32),
        compiler_params=_params("arbitrary"), name="post",
    )(x2d, o_m, o_d, o_s, w_gate, w_branch, w_out, ln_g, ln_b)


def _gelu(x):
    return 0.5 * x * (1.0 + lax.erf(x * (2.0 ** -0.5)))


FFN_CHUNKS = 2


def _ffn_chunks(dff):
    tiles = dff // LANES
    assert tiles * LANES == dff
    bounds = [LANES * ((tiles * i) // FFN_CHUNKS) for i in range(FFN_CHUNKS + 1)]
    return [(bounds[i], bounds[i + 1]) for i in range(FFN_CHUNKS) if bounds[i + 1] > bounds[i]]


def _ffn_kernel(*refs, alpha, tm, seq, has_past):
    if has_past:
        x_ref, ple_ref, p1_ref, p2_ref = refs[:4]
        refs = refs[4:]
    else:
        x_ref, ple_ref = refs[:2]
        refs = refs[2:]
    (wup_ref, cw_ref, cb_ref, wdn_ref, g_ref, b_ref, wple_ref, wpg_ref,
     y_ref, conv_ref, aext_ref) = refs
    dff = cw_ref.shape[1]
    ti = pl.program_id(1)

    @pl.when(ti == 0)
    def _():
        aext_ref[0:SUBLANES, :] = jnp.zeros((SUBLANES, dff), F32)

    x = x_ref[...]
    xb = x.astype(BF16)
    ffn = jnp.zeros(x.shape, F32)
    for c0, c1 in _ffn_chunks(dff):
        cs = slice(c0, c1)
        a = _nn(xb, wup_ref[:, c0:c1])
        gate_in = _nn(xb, wup_ref[:, dff + c0:dff + c1])
        aext_ref[SUBLANES:SUBLANES + tm, cs] = a
        a1 = aext_ref[SUBLANES - 1:SUBLANES - 1 + tm, cs]
        a2 = aext_ref[SUBLANES - 2:SUBLANES - 2 + tm, cs]
        if has_past:
            t = _iota((tm, 1), 0) % seq
            a1 = jnp.where(t >= 1, a1, p1_ref[:, cs])
            a2 = jnp.where(t >= 2, a2, p2_ref[:, cs])
            conv_ref[:, cs] = a
        else:
            aext_ref[0:SUBLANES, cs] = a[tm - SUBLANES:, :]
            conv_ref[:, cs] = a[tm - SUBLANES:, :]
        a_conv = cb_ref[:, cs] + (cw_ref[0:1, cs] * a2 + cw_ref[1:2, cs] * a1 + cw_ref[2:3, cs] * a)
        hid = _gelu(a_conv) * gate_in
        ffn = ffn + _nn(hid.astype(BF16), wdn_ref[c0:c1, :])
    y = _layer_norm(alpha * x + ffn, g_ref[...], b_ref[...])
    pg = _sigmoid(_nn(y.astype(BF16), wpg_ref[...]))
    y_ref[...] = y + pg * _nn(ple_ref[...].astype(BF16), wple_ref[...])


def _ffn(x2d, ple2d, past, w_up, conv_w, conv_b, w_down, ln_g, ln_b, w_ple, w_pg, alpha, tm, seq):
    m, d = x2d.shape
    dff = conv_w.shape[1]
    has_past = past is not None
    if has_past:
        assert m == tm and tm % seq == 0
        grid = (1, 1)
        nb = 1
    else:
        assert seq % tm == 0 and tm >= SUBLANES
        nb = seq // tm
        grid = (m // seq, nb)
    row = lambda i, j: (i * nb + j, 0)
    full = lambda a: pl.BlockSpec(a.shape, lambda i, j: (0,) * a.ndim)
    in_specs = [pl.BlockSpec((tm, d), row), pl.BlockSpec((tm, ple2d.shape[1]), row)]
    args = [x2d, ple2d]
    if has_past:
        in_specs += [pl.BlockSpec((tm, dff), row)] * 2
        args += list(past)
        conv_shape = jax.ShapeDtypeStruct((m, dff), F32)
        conv_spec = pl.BlockSpec((tm, dff), row)
    else:
        conv_shape = jax.ShapeDtypeStruct((m // seq, SUBLANES, dff), F32)
        conv_spec = pl.BlockSpec((None, SUBLANES, dff), lambda i, j: (i, 0, 0))
    weights = [w_up, conv_w, conv_b, w_down, ln_g, ln_b, w_ple, w_pg]
    in_specs += [full(w) for w in weights]
    return pl.pallas_call(
        functools.partial(_ffn_kernel, alpha=alpha, tm=tm, seq=seq, has_past=has_past),
        grid=grid, in_specs=in_specs,
        out_specs=[pl.BlockSpec((tm, d), row), conv_spec],
        out_shape=[jax.ShapeDtypeStruct((m, d), F32), conv_shape],
        scratch_shapes=[pltpu.VMEM((tm + SUBLANES, dff), F32)],
        compiler_params=_params("arbitrary", "arbitrary"), name="ffn",
    )(*args, *weights)


def kernel(x_prompt, x_sample, cache_k, cache_v, state_conv, page_table, p_prompt, p_sample, w_in, w_gate, w_branch, w_out, diff_lambda, diff_subln_g, ln1_g, ln1_b, w_up, conv_w, conv_b, w_down, ln2_g, ln2_b, w_ple, w_ple_gate):
    depth = w_in.shape[0]
    alpha = (2.0 * depth) ** 0.25
    bp, tp, d = x_prompt.shape
    bs, ts, _ = x_sample.shape
    dff = conv_w.shape[2]
    tm_p = 512 if tp % 512 == 0 else MOBA_BLOCK
    tm_f = 512 if tp % 512 == 0 else MOBA_BLOCK
    assert tp % MOBA_BLOCK == 0
    yp = x_prompt.reshape(bp * tp, d)
    ys = x_sample.reshape(bs * ts, d)
    outs = {k: [] for k in ("cp", "cs")}
    kv_p, kv_s = (), ()
    row2 = lambda a: a.reshape(1, -1)
    for i in range(depth):
        lam_init = 0.8 - 0.6 * math.exp(-0.3 * i)
        wi, wg, wbr, wo = (w[i].astype(BF16) for w in (w_in, w_gate, w_branch, w_out))
        wu, wd, wpl, wpg = (w[i].astype(BF16) for w in (w_up, w_down, w_ple, w_ple_gate))
        lam_p, subg = diff_lambda[i], row2(diff_subln_g[i])
        l1g, l1b, l2g, l2b, cb = (row2(a[i]) for a in (ln1_g, ln1_b, ln2_g, ln2_b, conv_b))
        cw = conv_w[i]

        *kv_p, qb, kb, vb, qmf, kmean = _proj(yp, wi, tm_p, True, i, depth, tuple(kv_p))
        to3 = lambda a: a.reshape(bp, tp, a.shape[-1])
        qb3, kb3, vb3 = to3(qb), to3(kb), to3(vb)
        o_m = _moba(qb3, to3(qmf), kmean.reshape(bp, tp // MOBA_BLOCK, MOBA_W), kb3, vb3)
        o_d = _diff(qb3, kb3, vb3, lam_p, subg.reshape(-1, 1), lam_init)
        o_s = _sb(qb3, kb3, vb3)
        x1 = _post(yp, o_m.reshape(-1, MOBA_W), o_d.reshape(-1, DIFF_W), o_s.reshape(-1, SB_W),
                   wg, wbr, wo, l1g, l1b, alpha, tm_f)
        yp, conv = _ffn(x1, p_prompt[i].reshape(bp * tp, -1), None, wu, cw, cb, wd, l2g, l2b, wpl, wpg,
                        alpha, tm_f, tp)
        outs["cp"].append(conv[:, SUBLANES - (CONV_W - 1):, :])

        ms = bs * ts
        *kv_s, qb, kb, vb, qmf = _proj(ys, wi, ms, False, i, depth, tuple(kv_s))
        to3 = lambda a: a.reshape(bs, ts, a.shape[-1])
        o = _sample_attn(i, to3(qb), to3(qmf), to3(kb), to3(vb), cache_k, cache_v, page_table,
                         lam_p, subg, lam_init).reshape(ms, -1).astype(BF16)
        x1 = _post(ys, o[:, :MOBA_W], o[:, MOBA_W:MOBA_W + DIFF_W], o[:, MOBA_W + DIFF_W:],
                   wg, wbr, wo, l1g, l1b, alpha, ms)
        st = state_conv[i]
        zeros = jnp.zeros((bs, ts - 1, dff), F32)
        p1 = jnp.concatenate([st[:, 1:2], zeros], axis=1).reshape(ms, dff)
        p2 = jnp.concatenate([st, zeros[:, 1:]], axis=1).reshape(ms, dff)
        ys, a_full = _ffn(x1, p_sample[i].reshape(ms, -1), (p1, p2), wu, cw, cb, wd, l2g, l2b, wpl, wpg,
                          alpha, ms, ts)
        outs["cs"].append(a_full.reshape(bs, ts, dff)[:, ts - (CONV_W - 1):, :])
    st = lambda name: jnp.stack(outs[name])
    k_p, v_p = (a.reshape(depth, bp, tp, -1) for a in kv_p)
    k_s, v_s = (a.reshape(depth, bs, ts, -1) for a in kv_s)
    return (yp.reshape(bp, tp, d), ys.reshape(bs, ts, d), k_p, v_p, st("cp"), k_s, v_s, st("cs"))
```

```python
import functools
import math

import numpy as np
import jax
import jax.numpy as jnp
from jax import lax
from jax.experimental import pallas as pl
from jax.experimental.pallas import tpu as pltpu

F32 = jnp.float32
BF16 = jnp.bfloat16

HEAD_DIM = 64
MOBA_HEADS = 4
MOBA_BLOCK = 256
MOBA_TOPK = 3
DIFF_HEADS = 4
SB_HEADS = 4
MOBA_W = MOBA_HEADS * HEAD_DIM
DIFF_W = DIFF_HEADS * 2 * HEAD_DIM
SB_W = SB_HEADS * HEAD_DIM
QK_W = MOBA_W + DIFF_W + SB_W
N_GROUPS = QK_W // HEAD_DIM
N_BRANCH = 3
N_ALIBI = MOBA_HEADS + DIFF_HEADS
CONV_W = 3
LN_EPS = 1e-5
QK_SCALE = HEAD_DIM ** -0.5
NEG = -1e30
LANES = 128
SUBLANES = 8
VMEM_LIMIT = 56 * 1024 * 1024
TQ = MOBA_BLOCK
GATE_ROWS = 2 * SUBLANES
DECODE_BLOCKS_PER_STEP = 4
DECODE_SLOTS = 3
SEQS_PER_STEP = 2
SB_DEAD_LOG2 = 160.0
DIFF_HEADS_PER_CHAIN = 2
ONES_ROWS = 2 * SUBLANES
LOG2E = math.log2(math.e)

_SLOPES = [2.0 ** (-8.0 * h / N_ALIBI) for h in range(1, N_ALIBI + 1)]
MOBA_SLOPES = _SLOPES[0::2]
DIFF_SLOPES = _SLOPES[1::2]


def _nt(a, b):
    return lax.dot_general(a, b, (((1,), (1,)), ((), ())), preferred_element_type=F32)


def _nn(a, b):
    return jnp.dot(a, b, preferred_element_type=F32)


def _split(x):
    hi = x.astype(BF16)
    lo = (x - hi.astype(F32)).astype(BF16)
    return hi, lo


def _sigmoid(x):
    return 1.0 / (1.0 + jnp.exp(-x))


def _softplus(z):
    return jnp.maximum(z, 0.0) + jnp.log(1.0 + jnp.exp(-jnp.abs(z)))


def _layer_norm(x, g, b):
    mu = jnp.mean(x, axis=-1, keepdims=True)
    xc = x - mu
    var = jnp.mean(xc * xc, axis=-1, keepdims=True)
    return xc * lax.rsqrt(var + LN_EPS) * g + b


def _params(*sem):
    return pltpu.CompilerParams(dimension_semantics=sem, vmem_limit_bytes=VMEM_LIMIT)


def _iota(shape, dim):
    return lax.broadcasted_iota(jnp.int32, shape, dim)


def _proj_kernel(*refs, n_blk, n_carried):
    x_ref, w_ref = refs[:2]
    k_ref, v_ref, qb_ref, kb_ref, vb_ref, qmf_ref, *rest = refs[2 + n_carried:]
    xb = x_ref[...].astype(BF16)
    q = _nn(xb, w_ref[:, 0:QK_W])
    k = _nn(xb, w_ref[:, QK_W:2 * QK_W])
    v = _nn(xb, w_ref[:, 2 * QK_W:3 * QK_W])
    if n_carried:
        k_ref[...] = k
        v_ref[...] = v
    else:
        k_ref[0] = k
        v_ref[0] = v
        for later in range(1, k_ref.shape[0]):
            k_ref[later] = jnp.zeros_like(k)
            v_ref[later] = jnp.zeros_like(v)
    qb_ref[...] = (q * QK_SCALE).astype(BF16)
    kb_ref[...] = k.astype(BF16)
    vb_ref[...] = v.astype(BF16)
    qmf_ref[...] = q[:, :MOBA_W]
    if n_blk:
        kmean_ref = rest[0]
        for j in range(n_blk):
            blk = k[j * MOBA_BLOCK:(j + 1) * MOBA_BLOCK, :MOBA_W]
            kmean_ref[j:j + 1, :] = jnp.sum(blk, axis=0, keepdims=True) * (1.0 / MOBA_BLOCK)


def _proj(x2d, w, tm, with_kmean, layer, depth, kv_all=()):
    m, d = x2d.shape
    n_blk = tm // MOBA_BLOCK if with_kmean else 0
    row = lambda i: (i, 0)
    out_shape = [jax.ShapeDtypeStruct((depth, m, QK_W), F32), jax.ShapeDtypeStruct((depth, m, QK_W), F32),
                 jax.ShapeDtypeStruct((m, QK_W), BF16), jax.ShapeDtypeStruct((m, QK_W), BF16),
                 jax.ShapeDtypeStruct((m, QK_W), BF16), jax.ShapeDtypeStruct((m, MOBA_W), F32)]
    assert bool(kv_all) == (layer > 0)
    kv_spec = (pl.BlockSpec((None, tm, QK_W), lambda i: (layer, i, 0)) if kv_all
               else pl.BlockSpec((depth, tm, QK_W), lambda i: (0, i, 0)))
    out_specs = ([kv_spec] * 2
                 + [pl.BlockSpec((tm, QK_W), row)] * 3 + [pl.BlockSpec((tm, MOBA_W), row)])
    if with_kmean:
        out_shape.append(jax.ShapeDtypeStruct((m // tm, n_blk, MOBA_W), F32))
        out_specs.append(pl.BlockSpec((None, n_blk, MOBA_W), lambda i: (i, 0, 0)))
    in_specs = [pl.BlockSpec((tm, d), row), pl.BlockSpec(w.shape, lambda i: (0, 0))]
    in_specs += [pl.BlockSpec(memory_space=pl.ANY)] * len(kv_all)
    return pl.pallas_call(
        functools.partial(_proj_kernel, n_blk=n_blk, n_carried=len(kv_all)),
        grid=(m // tm,),
        in_specs=in_specs, out_specs=out_specs, out_shape=out_shape,
        input_output_aliases={2 + i: i for i in range(len(kv_all))},
        compiler_params=_params("arbitrary"), name="proj",
    )(x2d, w, *kv_all)


def _fill_vt(v_ref, vt_ref, n_heads, width):
    t = v_ref.shape[0]
    extra = vt_ref.shape[1] - width
    for h in range(n_heads):
        if extra:
            vt_ref[h, width:, :] = jnp.ones((extra, t), BF16)
    for j in range(t // TQ):
        blk = v_ref[j * TQ:(j + 1) * TQ, :].astype(F32).T.astype(BF16)
        for h in range(n_heads):
            vt_ref[h, :width, j * TQ:(j + 1) * TQ] = blk[h * width:(h + 1) * width, :]


def _stack_heads_t(q_bf16, n_heads, width):
    qt = q_bf16.astype(F32).T * LOG2E
    row_head = _iota((n_heads * width, 1), 0) // width
    return jnp.concatenate([jnp.where(row_head == h, qt, 0.0).astype(BF16) for h in range(n_heads)], axis=1)


def _lane_slopes(slopes, n_lanes):
    lane_head = _iota((1, n_lanes), 1) // TQ
    out = jnp.zeros((1, n_lanes), F32)
    for h, s in enumerate(slopes):
        out = jnp.where(lane_head == h, s, out)
    return out


def _softmax_step(s, m, extra):
    cand = jnp.max(s, axis=0, keepdims=True) + extra
    m_new = jnp.maximum(m, cand)
    alpha = jnp.exp2(m - m_new)
    p = jnp.exp2(s - (m_new - extra))
    return p.astype(BF16), alpha, m_new


def _moba_kernel(q_ref, qf_ref, kmean_ref, k_ref, v_ref, o_ref, vt_ref, bias_ref, selb_ref, acc_ref, s_ref, *,
                 n_blk):
    nq = MOBA_HEADS * TQ
    qi = pl.program_id(1)

    slope_row = _lane_slopes(MOBA_SLOPES, nq) * LOG2E

    @pl.when(qi == 0)
    def _():
        _fill_vt(v_ref, vt_ref, MOBA_HEADS, HEAD_DIM)
        bias_ref[...] = _iota((TQ, nq), 0).astype(F32) * slope_row

    qt_all = _stack_heads_t(q_ref[...], MOBA_HEADS, HEAD_DIM)

    qft_hi, qft_lo = _split(qf_ref[...].T)
    km = kmean_ref[...]
    km = jnp.concatenate([km, jnp.zeros((GATE_ROWS - n_blk, MOBA_W), F32)], axis=0)
    lane_head = _iota((1, MOBA_W), 1) // HEAD_DIM
    blk_row = _iota((GATE_ROWS, 1), 0)
    for h in range(MOBA_HEADS):
        kh_hi, kh_lo = _split(jnp.where(lane_head == h, km, 0.0))
        g = _nn(kh_hi, qft_hi) + _nn(kh_hi, qft_lo) + _nn(kh_lo, qft_hi)
        cnt = jnp.zeros((GATE_ROWS, TQ), F32)
        for mb in range(n_blk - 1):
            gm = g[mb:mb + 1, :]
            tie = jnp.where(blk_row > mb, 1.0, 0.0)
            beats = jnp.where(gm > g, 1.0, jnp.where(gm == g, tie, 0.0))
            cnt = cnt + beats * jnp.where(mb < qi, 1.0, 0.0)
        sel = jnp.where(blk_row < qi, jnp.where(cnt < MOBA_TOPK, 0.0, NEG), NEG)
        selb_ref[:, h * TQ:(h + 1) * TQ] = sel

    acc_ref[...] = jnp.zeros_like(acc_ref)

    def scores(n, diag):
        s = _nn(k_ref[pl.ds(pl.multiple_of(n * TQ, TQ), TQ), :], qt_all) + bias_ref[...]
        if diag:
            s = jnp.where(_iota((TQ, nq), 0) <= _iota((TQ, nq), 1) % TQ, s, NEG)
        return s

    def attend(s, n, diag, m):
        start = pl.multiple_of(n * TQ, TQ)
        extra = slope_row * ((n - qi) * TQ).astype(F32)
        if not diag:
            extra = extra + selb_ref[pl.ds(n, 1), :]
        pb, alpha, m = _softmax_step(s, m, extra)
        for h in range(MOBA_HEADS):
            cols = slice(h * TQ, (h + 1) * TQ)
            pv = _nn(vt_ref[h, :, pl.ds(start, TQ)], pb[:, cols])
            acc_ref[h] = alpha[:, cols] * acc_ref[h] + pv
        return m

    def put(slot, n):
        s_ref[slot] = scores(n, False)

    def use(slot, n, m):
        return attend(s_ref[slot], n, False, m)

    put(0, 0)
    m = attend(scores(qi, True), qi, True, jnp.full((1, nq), NEG, F32))

    def body(u, m):
        put(1, 2 * u + 1)
        m = use(0, 2 * u, m)
        put(0, jnp.minimum(2 * u + 2, qi))
        return use(1, 2 * u + 1, m)

    m = lax.fori_loop(0, qi // 2, body, m)

    @pl.when(qi % 2 == 1)
    def _():
        use(0, qi - 1, m)

    outs = []
    for h in range(MOBA_HEADS):
        acc = acc_ref[h]
        outs.append(acc[:HEAD_DIM] * (1.0 / acc[HEAD_DIM:HEAD_DIM + 1]))
    o_ref[...] = jnp.concatenate(outs, axis=0).T.astype(o_ref.dtype)


def _moba(qb, qmf, kmean, kb, vb):
    b, t, _ = qb.shape
    n_blk = t // TQ
    assert n_blk <= GATE_ROWS
    nq = MOBA_HEADS * TQ
    return pl.pallas_call(
        functools.partial(_moba_kernel, n_blk=n_blk),
        grid=(b, n_blk),
        in_specs=[pl.BlockSpec((None, TQ, MOBA_W), lambda i, j: (i, j, 0)),
                  pl.BlockSpec((None, TQ, MOBA_W), lambda i, j: (i, j, 0)),
                  pl.BlockSpec((None, n_blk, MOBA_W), lambda i, j: (i, 0, 0)),
                  pl.BlockSpec((None, t, MOBA_W), lambda i, j: (i, 0, 0)),
                  pl.BlockSpec((None, t, MOBA_W), lambda i, j: (i, 0, 0))],
        out_specs=pl.BlockSpec((None, TQ, MOBA_W), lambda i, j: (i, j, 0)),
        out_shape=jax.ShapeDtypeStruct((b, t, MOBA_W), BF16),
        scratch_shapes=[pltpu.VMEM((MOBA_HEADS, HEAD_DIM + ONES_ROWS, t), BF16), pltpu.VMEM((TQ, nq), F32),
                        pltpu.VMEM((GATE_ROWS, nq), F32),
                        pltpu.VMEM((MOBA_HEADS, HEAD_DIM + ONES_ROWS, TQ), F32),
                        pltpu.VMEM((2, TQ, nq), F32)],
        compiler_params=_params("arbitrary", "arbitrary"), name="moba",
    )(qb, qmf, kmean, kb, vb)


def _suffix_matrix(n, transposed):
    r = _iota((n, n), 0)
    c = _iota((n, n), 1)
    return jnp.where((c > r) if transposed else (r > c), 1.0, 0.0).astype(BF16)


def _sb_kernel(q_ref, k_ref, v_ref, o_ref, vt_ref, acc_ref):
    nq = SB_HEADS * TQ
    nc = q_ref.shape[0]
    qi = pl.program_id(1)

    @pl.when(qi == 0)
    def _():
        for c in range(nc):
            _fill_vt(v_ref.at[c], vt_ref.at[c], SB_HEADS, HEAD_DIM)

    qt_all = [_stack_heads_t(q_ref[c], SB_HEADS, HEAD_DIM) for c in range(nc)]
    r = _iota((TQ + ONES_ROWS, 2 * TQ), 0)
    c = _iota((TQ + ONES_ROWS, 2 * TQ), 1) % TQ
    later_mat = jnp.where(r >= TQ, -1.0, jnp.where(c > r, -1.0, 0.0)).astype(BF16)
    acc_ref[...] = jnp.zeros_like(acc_ref)

    def scores(c, n):
        return _nn(k_ref[c, pl.ds(pl.multiple_of(n * TQ, TQ), TQ), :], qt_all[c])

    def attend(c, z, n, diag, cs):
        start = pl.multiple_of(n * TQ, TQ)
        sp = jnp.maximum(z, 0.0) + jnp.log(1.0 + jnp.exp2(-jnp.abs(z))) * LOG2E
        if diag:
            before = _iota((TQ, nq), 0) < _iota((TQ, nq), 1) % TQ
            sp_keep = jnp.where(before, sp, 0.0)
        else:
            sp_keep = sp
        sums = _nn(later_mat, jnp.concatenate(_split(sp_keep), axis=0))
        a = jnp.exp2((z - sp) + sums[:TQ] + cs)
        if diag:
            a = jnp.where(before, a, 0.0)
        ab = a.astype(BF16)
        for h in range(SB_HEADS):
            acc_ref[c, h] += _nn(vt_ref[c, h, :, pl.ds(start, TQ)], ab[:, h * TQ:(h + 1) * TQ])
        return cs + sums[TQ:TQ + 1]

    def step(n, diag, css):
        return tuple(attend(c, scores(c, n), n, diag, css[c]) for c in range(nc))

    css = step(qi, True, tuple(jnp.zeros((1, nq), F32) for _ in range(nc)))

    def alive(carry):
        i, css = carry
        top = jnp.max(css[0])
        for c in range(1, nc):
            top = jnp.maximum(top, jnp.max(css[c]))
        return jnp.logical_and(i < qi, top > -SB_DEAD_LOG2)

    lax.while_loop(alive, lambda carry: (carry[0] + 1, step(qi - 1 - carry[0], False, carry[1])),
                   (jnp.int32(0), css))
    for c in range(nc):
        o_ref[c] = jnp.concatenate([acc_ref[c, h] for h in range(SB_HEADS)], axis=0).T.astype(o_ref.dtype)


def _sb(qb, kb, vb):
    b, t, _ = qb.shape
    cb = (QK_W - SB_W) // SB_W
    nc = math.gcd(b, SEQS_PER_STEP)
    return pl.pallas_call(
        _sb_kernel,
        grid=(b // nc, t // TQ),
        in_specs=[pl.BlockSpec((nc, TQ, SB_W), lambda i, j: (i, j, cb)),
                  pl.BlockSpec((nc, t, SB_W), lambda i, j: (i, 0, cb)),
                  pl.BlockSpec((nc, t, SB_W), lambda i, j: (i, 0, cb))],
        out_specs=pl.BlockSpec((nc, TQ, SB_W), lambda i, j: (i, j, 0)),
        out_shape=jax.ShapeDtypeStruct((b, t, SB_W), BF16),
        scratch_shapes=[pltpu.VMEM((nc, SB_HEADS, HEAD_DIM, t), BF16),
                        pltpu.VMEM((nc, SB_HEADS, HEAD_DIM, TQ), F32)],
        compiler_params=_params("arbitrary", "arbitrary"), name="sb",
    )(qb, kb, vb)


def _diff_lambda(lam_ref, lam_init):
    dl = lam_ref[...]
    s1 = jnp.sum(dl[0:1] * dl[1:2], axis=1, keepdims=True)
    s2 = jnp.sum(dl[2:3] * dl[3:4], axis=1, keepdims=True)
    return jnp.exp(s1) - jnp.exp(s2) + lam_init


def _diff_kernel(*refs, lam_init):
    hd = 2 * HEAD_DIM
    nh = DIFF_HEADS_PER_CHAIN
    nc = DIFF_HEADS // nh
    nq = nh * 2 * TQ
    q_refs, k_refs, v_refs = refs[:nc], refs[nc:2 * nc], refs[2 * nc:3 * nc]
    lam_ref, g_ref, o_ref, vt_ref, bias_ref, acc_ref, s_ref = refs[3 * nc:]
    qi = pl.program_id(1)
    slope_rows = []
    for c in range(nc):
        lane_head = _iota((1, nq), 1) // (2 * TQ) + c * nh
        row = jnp.zeros((1, nq), F32)
        for i in range(DIFF_HEADS):
            row = jnp.where(lane_head == i, DIFF_SLOPES[i] * LOG2E, row)
        slope_rows.append(row)

    @pl.when(qi == 0)
    def _():
        for c in range(nc):
            _fill_vt(v_refs[c], vt_ref.at[c], nh, hd)
            bias_ref[c] = _iota((TQ, nq), 0).astype(F32) * slope_rows[c]

    qt_all = [_stack_heads_t(q_refs[c][...], 2 * nh, HEAD_DIM) for c in range(nc)]
    acc_ref[...] = jnp.zeros_like(acc_ref)

    def scores(c, n, diag):
        s = _nn(k_refs[c][pl.ds(pl.multiple_of(n * TQ, TQ), TQ), :], qt_all[c]) + bias_ref[c]
        if diag:
            s = jnp.where(_iota((TQ, nq), 0) <= _iota((TQ, nq), 1) % TQ, s, NEG)
        return s

    def attend(c, s, n, m):
        start = pl.multiple_of(n * TQ, TQ)
        extra = slope_rows[c] * ((n - qi) * TQ).astype(F32)
        pb, alpha, m = _softmax_step(s, m, extra)
        for h in range(nh):
            cols = slice(h * 2 * TQ, (h + 1) * 2 * TQ)
            pv = _nn(vt_ref[c, h, :, pl.ds(start, TQ)], pb[:, cols])
            acc_ref[c, h] = alpha[:, cols] * acc_ref[c, h] + pv
        return m

    def put(slot, n):
        for c in range(nc):
            s_ref[slot, c] = scores(c, n, False)

    def use(slot, n, ms):
        return tuple(attend(c, s_ref[slot, c], n, ms[c]) for c in range(nc))

    put(0, 0)
    ms = tuple(attend(c, scores(c, qi, True), qi, jnp.full((1, nq), NEG, F32)) for c in range(nc))

    def body(u, ms):
        put(1, 2 * u + 1)
        ms = use(0, 2 * u, ms)
        put(0, jnp.minimum(2 * u + 2, qi))
        return use(1, 2 * u + 1, ms)

    ms = lax.fori_loop(0, qi // 2, body, ms)

    @pl.when(qi % 2 == 1)
    def _():
        use(0, qi - 1, ms)

    lam = _diff_lambda(lam_ref, lam_init)
    outs = []
    for c in range(nc):
        for h in range(nh):
            acc = acc_ref[c, h]
            acc = acc[:hd] * (1.0 / acc[hd:hd + 1])
            o = acc[:, :TQ] - lam * acc[:, TQ:]
            ms2 = jnp.mean(o * o, axis=0, keepdims=True)
            outs.append(o * lax.rsqrt(ms2 + LN_EPS) * g_ref[...] * (1.0 - lam_init))
    o_ref[...] = jnp.concatenate(outs, axis=0).T.astype(o_ref.dtype)


def _diff(qb, kb, vb, lam_p, subln_g_col, lam_init):
    b, t, _ = qb.shape
    nh = DIFF_HEADS_PER_CHAIN
    nc = DIFF_HEADS // nh
    w = nh * 2 * HEAD_DIM
    assert MOBA_W % w == 0 and DIFF_HEADS % nh == 0
    c0 = MOBA_W // w
    tile = lambda c: pl.BlockSpec((None, TQ, w), lambda i, j: (i, j, c0 + c))
    seq = lambda c: pl.BlockSpec((None, t, w), lambda i, j: (i, 0, c0 + c))
    rows = 2 * HEAD_DIM + ONES_ROWS
    return pl.pallas_call(
        functools.partial(_diff_kernel, lam_init=lam_init),
        grid=(b, t // TQ),
        in_specs=([tile(c) for c in range(nc)] + [seq(c) for c in range(nc)] * 2
                  + [pl.BlockSpec(lam_p.shape, lambda i, j: (0, 0)),
                     pl.BlockSpec(subln_g_col.shape, lambda i, j: (0, 0))]),
        out_specs=pl.BlockSpec((None, TQ, DIFF_W), lambda i, j: (i, j, 0)),
        out_shape=jax.ShapeDtypeStruct((b, t, DIFF_W), BF16),
        scratch_shapes=[pltpu.VMEM((nc, nh, rows, t), BF16),
                        pltpu.VMEM((nc, TQ, nh * 2 * TQ), F32),
                        pltpu.VMEM((nc, nh, rows, 2 * TQ), F32),
                        pltpu.VMEM((2, nc, TQ, nh * 2 * TQ), F32)],
        compiler_params=_params("arbitrary", "arbitrary"), name="diff",
    )(*([qb] * nc), *([kb] * nc), *([vb] * nc), lam_p, subln_g_col)


def _sample_attn_kernel(pt_ref, qbd_ref, qg_ref, kn_ref, vn_ref, lam_ref, g_ref, ck_ref, cv_ref, o_ref,
                        acc_ref, m_ref, l_ref, cs_ref, gall_ref, mall_ref, lall_ref, oall_ref,
                        kbuf_ref, vbuf_ref, sem_ref, *,
                        layer, n_blk, tq, past_len, lam_init, n_pages_step, blocks_per_step):
    rows = tq * N_GROUPS
    b = pl.program_id(0)
    j = pl.program_id(1)
    last = pl.num_programs(1) - 1
    page = kbuf_ref.shape[2]
    ppb = MOBA_BLOCK // page
    n_chunks = n_blk // blocks_per_step
    total_chunks = pl.num_programs(0) * n_chunks

    def chunk_copies(c):
        slot = c % DECODE_SLOTS
        seq = c // n_chunks
        first_page = (n_blk - (c % n_chunks + 1) * blocks_per_step) * ppb
        copies = []
        for p in range(n_pages_step):
            pid = pt_ref[seq, first_page + p]
            copies.append(pltpu.make_async_copy(ck_ref.at[layer, pid], kbuf_ref.at[slot, p],
                                                sem_ref.at[slot, p]))
            copies.append(pltpu.make_async_copy(cv_ref.at[layer, pid], vbuf_ref.at[slot, p],
                                                sem_ref.at[slot, n_pages_step + p]))
        return copies

    def start_chunk(c):
        @pl.when(c < total_chunks)
        def _():
            for cp in chunk_copies(c):
                cp.start()

    @pl.when((b == 0) & (j == 0))
    def _():
        for c in range(DECODE_SLOTS - 1):
            start_chunk(c)

    row = _iota((rows, 1), 0)
    grp = row % N_GROUPS
    qidx = row // N_GROUPS
    kind = jnp.where(grp < MOBA_HEADS, 0, jnp.where(grp < MOBA_HEADS + 2 * DIFF_HEADS, 1, 2))
    is_moba = kind == 0
    is_diff = kind == 1
    is_sb = kind == 2
    slope = jnp.zeros((rows, 1), F32)
    for h in range(MOBA_HEADS):
        slope = jnp.where(grp == h, MOBA_SLOPES[h], slope)
    for h in range(DIFF_HEADS):
        slope = jnp.where(grp - MOBA_HEADS - 2 * h == 0, DIFF_SLOPES[h], slope)
        slope = jnp.where(grp - MOBA_HEADS - 2 * h == 1, DIFF_SLOPES[h], slope)
    blk_lane = _iota((rows, LANES), 1)

    @pl.when(j == 0)
    def _():
        acc_ref[...] = jnp.zeros_like(acc_ref)
        m_ref[...] = jnp.full_like(m_ref, NEG)
        l_ref[...] = jnp.zeros_like(l_ref)
        cs_ref[...] = jnp.zeros_like(cs_ref)
        gall_ref[...] = jnp.zeros_like(gall_ref)
        mall_ref[...] = jnp.full_like(mall_ref, NEG)
        lall_ref[...] = jnp.zeros_like(lall_ref)

    def process(kb, vb, seg, n_first, pos0, limit):
        nk = kb.shape[0]
        n_seg = nk // seg
        s_raw = _nt(qbd_ref[...], kb)
        col = _iota((1, nk), 1)
        s = s_raw + slope * (col.astype(F32) + pos0)
        if limit is not None:
            valid = col < limit
            s = jnp.where(valid, s, NEG)
        segs = [slice(i * seg, (i + 1) * seg) for i in range(n_seg)]
        seg_max = [jnp.max(s[:, sl], axis=1, keepdims=True) for sl in segs]
        m_old = m_ref[...]
        m_new = m_old
        for sm in seg_max:
            m_new = jnp.maximum(m_new, sm)
        alpha = jnp.exp(m_old - m_new)
        sp = _softplus(s_raw)
        lk = -sp
        if limit is not None:
            lk = jnp.where(valid, lk, 0.0)
        later_mat = _suffix_matrix(seg, False)
        run = cs_ref[...]
        a_segs = [None] * n_seg
        for i in reversed(range(n_seg)):
            lk_hi, lk_lo = _split(lk[:, segs[i]])
            later = _nn(lk_hi, later_mat) + _nn(lk_lo, later_mat) + run
            a_segs[i] = jnp.exp((s_raw[:, segs[i]] - sp[:, segs[i]]) + later)
            run = run + jnp.sum(lk[:, segs[i]], axis=1, keepdims=True)
        cs_ref[...] = run
        psum = jnp.zeros((rows, 1), F32)
        p_segs = []
        mall, lall = mall_ref[...], lall_ref[...]
        for i in range(n_seg):
            p = jnp.exp(s[:, segs[i]] - jnp.where(is_moba, seg_max[i], m_new))
            ps = jnp.sum(p, axis=1, keepdims=True)
            psum = psum + ps
            mall = jnp.where(blk_lane == n_first + i, seg_max[i], mall)
            lall = jnp.where(blk_lane == n_first + i, ps, lall)
            a = a_segs[i]
            if limit is not None:
                a = jnp.where(valid[:, segs[i]], a, 0.0)
            p_segs.append(jnp.where(is_sb, a, p).astype(BF16))
        mall_ref[...] = mall
        lall_ref[...] = lall
        l_ref[...] = jnp.where(is_diff, alpha * l_ref[...] + psum, l_ref[...])
        m_ref[...] = jnp.where(is_diff, m_new, m_old)
        for i in range(n_seg):
            oall_ref[n_first + i] = _nn(p_segs[i], vb[segs[i], :MOBA_W])
        pmat = p_segs[0] if n_seg == 1 else jnp.concatenate(p_segs, axis=1)
        fac = jnp.where(is_diff, alpha, jnp.where(is_sb, 1.0, 0.0))
        acc_ref[...] = acc_ref[...] * fac + _nn(pmat, vb[:, MOBA_W:])

    @pl.when(j == 0)
    def _():
        limit = jnp.where(is_sb, qidx, qidx + 1)
        process(kn_ref[...], vn_ref[...], kn_ref.shape[0], n_blk, 0.0, limit)

    @pl.when(j > 0)
    def _():
        chunk = b * n_chunks + j - 1
        for cp in chunk_copies(chunk):
            cp.wait()
        start_chunk(chunk + DECODE_SLOTS - 1)
        slot = chunk % DECODE_SLOTS
        n_first = n_blk - j * blocks_per_step
        gall = gall_ref[...]
        for i in range(blocks_per_step):
            ksum = jnp.zeros((1, MOBA_W), F32)
            for p in range(ppb):
                ksum = ksum + jnp.sum(kbuf_ref[slot, i * ppb + p, :, :MOBA_W], axis=0, keepdims=True)
            gate = jnp.sum(qg_ref[...] * (ksum * (1.0 / MOBA_BLOCK)), axis=1, keepdims=True)
            gall = jnp.where(blk_lane == n_first + i, gate, gall)
        gall_ref[...] = gall
        kb = jnp.concatenate([kbuf_ref[slot, p].astype(BF16) for p in range(n_pages_step)], axis=0)
        vb = jnp.concatenate([vbuf_ref[slot, p].astype(BF16) for p in range(n_pages_step)], axis=0)
        pos0 = (n_first * MOBA_BLOCK - past_len).astype(F32)
        process(kb, vb, MOBA_BLOCK, n_first, pos0, None)

    @pl.when(j == last)
    def _():
        g = gall_ref[...]
        cnt = jnp.zeros((rows, LANES), F32)
        for mb in range(n_blk):
            gm = g[:, mb:mb + 1]
            tie = jnp.where(blk_lane > mb, 1.0, 0.0)
            cnt = cnt + jnp.where(gm > g, 1.0, jnp.where(gm == g, tie, 0.0))
        rank_lim = jnp.where(blk_lane < n_blk, float(MOBA_TOPK), jnp.where(blk_lane == n_blk, 1e9, -1.0))
        sel = cnt < rank_lim
        mall = jnp.where(sel, mall_ref[...], NEG)
        m_tot = jnp.max(mall, axis=1, keepdims=True)
        w = jnp.where(sel, jnp.exp(mall - m_tot), 0.0)
        l_tot = jnp.sum(w * lall_ref[...], axis=1, keepdims=True)
        o_m = jnp.zeros((rows, MOBA_W), F32)
        for nb in range(n_blk + 1):
            o_m = o_m + w[:, nb:nb + 1] * oall_ref[nb]
        o_m = o_m * jnp.where(is_moba, 1.0 / l_tot, 0.0)
        rest = acc_ref[...] * jnp.where(is_diff, 1.0 / l_ref[...], jnp.where(is_sb, 1.0, 0.0))
        acc = jnp.concatenate([o_m, rest], axis=1)
        lam = _diff_lambda(lam_ref, lam_init)
        g16 = _iota((N_GROUPS, 1), 0)
        cgrp = _iota((1, QK_W), 1) // HEAD_DIM
        d_lo, d_hi = MOBA_HEADS, MOBA_HEADS + 2 * DIFF_HEADS
        g_diff = jnp.where(g16 >= d_lo, jnp.where(g16 < d_hi, 1, 0), 0)
        c_diff = jnp.where(cgrp >= d_lo, jnp.where(cgrp < d_hi, 1, 0), 0)
        g_key = jnp.where(g_diff == 1, d_lo + 2 * ((g16 - d_lo) // 2), g16)
        c_key = jnp.where(c_diff == 1, d_lo + 2 * ((cgrp - d_lo) // 2), cgrp)
        second_half = g_diff * ((g16 - d_lo) % 2)
        coef = jnp.where(g_key == c_key, jnp.where(second_half == 1, -lam, 1.0), 0.0)
        hd = 2 * HEAD_DIM
        gsub = g_ref[...]
        for t in range(tq):
            o_t = jnp.sum(acc[t * N_GROUPS:(t + 1) * N_GROUPS, :] * coef, axis=0, keepdims=True)
            pieces = [o_t[:, :MOBA_W]]
            for h in range(DIFF_HEADS):
                od = o_t[:, MOBA_W + h * hd:MOBA_W + (h + 1) * hd]
                ms2 = jnp.mean(od * od, axis=1, keepdims=True)
                pieces.append(od * lax.rsqrt(ms2 + LN_EPS) * gsub * (1.0 - lam_init))
            pieces.append(o_t[:, MOBA_W + DIFF_W:])
            o_ref[t:t + 1, :] = jnp.concatenate(pieces, axis=1)


def _sample_attn(layer, qb, qmf, kb_new, vb_new, cache_k, cache_v, page_table, lam_p, subln_g, lam_init):
    b, tq, _ = qb.shape
    page = cache_k.shape[2]
    n_pages = page_table.shape[1]
    past_len = n_pages * page
    assert past_len % MOBA_BLOCK == 0 and MOBA_BLOCK % page == 0 and tq <= SUBLANES
    n_blk = past_len // MOBA_BLOCK
    assert n_blk + 1 <= LANES
    bps = math.gcd(n_blk, DECODE_BLOCKS_PER_STEP)
    pps = bps * (MOBA_BLOCK // page)
    rows = tq * N_GROUPS
    n_new = LANES

    gmask = (np.arange(QK_W)[None, :] // HEAD_DIM == np.arange(N_GROUPS)[:, None])
    qbd = jnp.where(gmask[None, None], qb[:, :, None, :], jnp.zeros((), BF16)).reshape(b, rows, QK_W)
    qg = jnp.where(gmask[None, None, :, :MOBA_W], qmf[:, :, None, :], 0.0).reshape(b, rows, MOBA_W)
    pad = ((0, 0), (0, n_new - tq), (0, 0))
    kn = jnp.pad(kb_new, pad)
    vn = jnp.pad(vb_new, pad)

    per_b = lambda i, j, pt: (i, 0, 0)
    whole = lambda i, j, pt: (0, 0)
    in_specs = [pl.BlockSpec((None, rows, QK_W), per_b),
                pl.BlockSpec((None, rows, MOBA_W), per_b),
                pl.BlockSpec((None, n_new, QK_W), per_b),
                pl.BlockSpec((None, n_new, QK_W), per_b),
                pl.BlockSpec(lam_p.shape, whole),
                pl.BlockSpec(subln_g.shape, whole),
                pl.BlockSpec(memory_space=pl.ANY),
                pl.BlockSpec(memory_space=pl.ANY)]
    grid_spec = pltpu.PrefetchScalarGridSpec(
        num_scalar_prefetch=1, grid=(b, n_blk // bps + 1), in_specs=in_specs,
        out_specs=pl.BlockSpec((None, tq, QK_W), per_b),
        scratch_shapes=[pltpu.VMEM((rows, QK_W - MOBA_W), F32), pltpu.VMEM((rows, 1), F32),
                        pltpu.VMEM((rows, 1), F32), pltpu.VMEM((rows, 1), F32),
                        pltpu.VMEM((rows, LANES), F32), pltpu.VMEM((rows, LANES), F32),
                        pltpu.VMEM((rows, LANES), F32), pltpu.VMEM((n_blk + 1, rows, MOBA_W), F32),
                        pltpu.VMEM((DECODE_SLOTS, pps, page, QK_W), F32),
                        pltpu.VMEM((DECODE_SLOTS, pps, page, QK_W), F32),
                        pltpu.SemaphoreType.DMA((DECODE_SLOTS, 2 * pps))])
    return pl.pallas_call(
        functools.partial(_sample_attn_kernel, layer=layer, n_blk=n_blk, tq=tq, past_len=past_len,
                          lam_init=lam_init, n_pages_step=pps, blocks_per_step=bps),
        grid_spec=grid_spec,
        out_shape=jax.ShapeDtypeStruct((b, tq, QK_W), F32),
        compiler_params=_params("arbitrary", "arbitrary"), name="sample_attn",
    )(page_table, qbd, qg, kn, vn, lam_p, subln_g, cache_k, cache_v)


def _row_chains(tm):
    half = tm // 2
    if half % MOBA_BLOCK:
        return [slice(0, tm)]
    return [slice(0, half), slice(half, tm)]


def _post_kernel(x_ref, om_ref, od_ref, os_ref, wg_ref, wb_ref, wo_ref, g_ref, b_ref, y_ref, *, alpha):
    tm, d = x_ref.shape
    bounds = (0, MOBA_W, MOBA_W + DIFF_W, MOBA_W + DIFF_W + SB_W)
    for rows in _row_chains(tm):
        x = x_ref[rows, :]
        xb = x.astype(BF16)
        merged = jnp.zeros(x.shape, F32)
        for i, o_ref in enumerate((om_ref, od_ref, os_ref)):
            gate = _sigmoid(_nn(xb, wg_ref[:, i * d:(i + 1) * d]))
            merged = merged + gate * _nn(o_ref[rows, :], wb_ref[bounds[i]:bounds[i + 1], :])
        mix = _nn(merged.astype(BF16), wo_ref[...])
        y_ref[rows, :] = _layer_norm(alpha * x + mix, g_ref[...], b_ref[...])


def _post(x2d, o_m, o_d, o_s, w_gate, w_branch, w_out, ln_g, ln_b, alpha, tm):
    m, d = x2d.shape
    row = lambda i: (i, 0)
    full = lambda a: pl.BlockSpec(a.shape, lambda i: (0, 0))
    return pl.pallas_call(
        functools.partial(_post_kernel, alpha=alpha),
        grid=(m // tm,),
        in_specs=[pl.BlockSpec((tm, d), row), pl.BlockSpec((tm, MOBA_W), row),
                  pl.BlockSpec((tm, DIFF_W), row), pl.BlockSpec((tm, SB_W), row),
                  full(w_gate), full(w_branch), full(w_out), full(ln_g), full(ln_b)],
        out_specs=pl.BlockSpec((tm, d), row),
        out_shape=jax.ShapeDtypeStruct((m, d), F32),
        compiler_params=_params("arbitrary"), name="post",
    )(x2d, o_m, o_d, o_s, w_gate, w_branch, w_out, ln_g, ln_b)


def _gelu(x):
    return 0.5 * x * (1.0 + lax.erf(x * (2.0 ** -0.5)))


FFN_CHUNKS = 2


def _ffn_chunks(dff):
    tiles = dff // LANES
    assert tiles * LANES == dff
    bounds = [LANES * ((tiles * i) // FFN_CHUNKS) for i in range(FFN_CHUNKS + 1)]
    return [(bounds[i], bounds[i + 1]) for i in range(FFN_CHUNKS) if bounds[i + 1] > bounds[i]]


def _ffn_kernel(*refs, alpha, tm, seq, has_past):
    if has_past:
        x_ref, ple_ref, p1_ref, p2_ref = refs[:4]
        refs = refs[4:]
    else:
        x_ref, ple_ref = refs[:2]
        refs = refs[2:]
    (wup_ref, cw_ref, cb_ref, wdn_ref, g_ref, b_ref, wple_ref, wpg_ref,
     y_ref, conv_ref, aext_ref) = refs
    dff = cw_ref.shape[1]
    ti = pl.program_id(1)

    @pl.when(ti == 0)
    def _():
        aext_ref[0:SUBLANES, :] = jnp.zeros((SUBLANES, dff), F32)

    x = x_ref[...]
    xb = x.astype(BF16)
    ffn = jnp.zeros(x.shape, F32)
    for c0, c1 in _ffn_chunks(dff):
        cs = slice(c0, c1)
        a = _nn(xb, wup_ref[:, c0:c1])
        gate_in = _nn(xb, wup_ref[:, dff + c0:dff + c1])
        aext_ref[SUBLANES:SUBLANES + tm, cs] = a
        a1 = aext_ref[SUBLANES - 1:SUBLANES - 1 + tm, cs]
        a2 = aext_ref[SUBLANES - 2:SUBLANES - 2 + tm, cs]
        if has_past:
            t = _iota((tm, 1), 0) % seq
            a1 = jnp.where(t >= 1, a1, p1_ref[:, cs])
            a2 = jnp.where(t >= 2, a2, p2_ref[:, cs])
            conv_ref[:, cs] = a
        else:
            aext_ref[0:SUBLANES, cs] = a[tm - SUBLANES:, :]
            conv_ref[:, cs] = a[tm - SUBLANES:, :]
        a_conv = cb_ref[:, cs] + (cw_ref[0:1, cs] * a2 + cw_ref[1:2, cs] * a1 + cw_ref[2:3, cs] * a)
        hid = _gelu(a_conv) * gate_in
        ffn = ffn + _nn(hid.astype(BF16), wdn_ref[c0:c1, :])
    y = _layer_norm(alpha * x + ffn, g_ref[...], b_ref[...])
    pg = _sigmoid(_nn(y.astype(BF16), wpg_ref[...]))
    y_ref[...] = y + pg * _nn(ple_ref[...].astype(BF16), wple_ref[...])


def _ffn(x2d, ple2d, past, w_up, conv_w, conv_b, w_down, ln_g, ln_b, w_ple, w_pg, alpha, tm, seq):
    m, d = x2d.shape
    dff = conv_w.shape[1]
    has_past = past is not None
    if has_past:
        assert m == tm and tm % seq == 0
        grid = (1, 1)
        nb = 1
    else:
        assert seq % tm == 0 and tm >= SUBLANES
        nb = seq // tm
        grid = (m // seq, nb)
    row = lambda i, j: (i * nb + j, 0)
    full = lambda a: pl.BlockSpec(a.shape, lambda i, j: (0,) * a.ndim)
    in_specs = [pl.BlockSpec((tm, d), row), pl.BlockSpec((tm, ple2d.shape[1]), row)]
    args = [x2d, ple2d]
    if has_past:
        in_specs += [pl.BlockSpec((tm, dff), row)] * 2
        args += list(past)
        conv_shape = jax.ShapeDtypeStruct((m, dff), F32)
        conv_spec = pl.BlockSpec((tm, dff), row)
    else:
        conv_shape = jax.ShapeDtypeStruct((m // seq, SUBLANES, dff), F32)
        conv_spec = pl.BlockSpec((None, SUBLANES, dff), lambda i, j: (i, 0, 0))
    weights = [w_up, conv_w, conv_b, w_down, ln_g, ln_b, w_ple, w_pg]
    in_specs += [full(w) for w in weights]
    return pl.pallas_call(
        functools.partial(_ffn_kernel, alpha=alpha, tm=tm, seq=seq, has_past=has_past),
        grid=grid, in_specs=in_specs,
        out_specs=[pl.BlockSpec((tm, d), row), conv_spec],
        out_shape=[jax.ShapeDtypeStruct((m, d), F32), conv_shape],
        scratch_shapes=[pltpu.VMEM((tm + SUBLANES, dff), F32)],
        compiler_params=_params("arbitrary", "arbitrary"), name="ffn",
    )(*args, *weights)


def kernel(x_prompt, x_sample, cache_k, cache_v, state_conv, page_table, p_prompt, p_sample, w_in, w_gate, w_branch, w_out, diff_lambda, diff_subln_g, ln1_g, ln1_b, w_up, conv_w, conv_b, w_down, ln2_g, ln2_b, w_ple, w_ple_gate):
    depth = w_in.shape[0]
    alpha = (2.0 * depth) ** 0.25
    bp, tp, d = x_prompt.shape
    bs, ts, _ = x_sample.shape
    dff = conv_w.shape[2]
    tm_p = 512 if tp % 512 == 0 else MOBA_BLOCK
    tm_f = 512 if tp % 512 == 0 else MOBA_BLOCK
    assert tp % MOBA_BLOCK == 0
    yp = x_prompt.reshape(bp * tp, d)
    ys = x_sample.reshape(bs * ts, d)
    outs = {k: [] for k in ("cp", "cs")}
    kv_p, kv_s = (), ()
    row2 = lambda a: a.reshape(1, -1)
    for i in range(depth):
        lam_init = 0.8 - 0.6 * math.exp(-0.3 * i)
        wi, wg, wbr, wo = (w[i].astype(BF16) for w in (w_in, w_gate, w_branch, w_out))
        wu, wd, wpl, wpg = (w[i].astype(BF16) for w in (w_up, w_down, w_ple, w_ple_gate))
        lam_p, subg = diff_lambda[i], row2(diff_subln_g[i])
        l1g, l1b, l2g, l2b, cb = (row2(a[i]) for a in (ln1_g, ln1_b, ln2_g, ln2_b, conv_b))
        cw = conv_w[i]

        *kv_p, qb, kb, vb, qmf, kmean = _proj(yp, wi, tm_p, True, i, depth, tuple(kv_p))
        to3 = lambda a: a.reshape(bp, tp, a.shape[-1])
        qb3, kb3, vb3 = to3(qb), to3(kb), to3(vb)
        o_m = _moba(qb3, to3(qmf), kmean.reshape(bp, tp // MOBA_BLOCK, MOBA_W), kb3, vb3)
        o_d = _diff(qb3, kb3, vb3, lam_p, subg.reshape(-1, 1), lam_init)
        o_s = _sb(qb3, kb3, vb3)
        x1 = _post(yp, o_m.reshape(-1, MOBA_W), o_d.reshape(-1, DIFF_W), o_s.reshape(-1, SB_W),
                   wg, wbr, wo, l1g, l1b, alpha, tm_f)
        yp, conv = _ffn(x1, p_prompt[i].reshape(bp * tp, -1), None, wu, cw, cb, wd, l2g, l2b, wpl, wpg,
                        alpha, tm_f, tp)
        outs["cp"].append(conv[:, SUBLANES - (CONV_W - 1):, :])

        ms = bs * ts
        *kv_s, qb, kb, vb, qmf = _proj(ys, wi, ms, False, i, depth, tuple(kv_s))
        to3 = lambda a: a.reshape(bs, ts, a.shape[-1])
        o = _sample_attn(i, to3(qb), to3(qmf), to3(kb), to3(vb), cache_k, cache_v, page_table,
                         lam_p, subg, lam_init).reshape(ms, -1).astype(BF16)
        x1 = _post(ys, o[:, :MOBA_W], o[:, MOBA_W:MOBA_W + DIFF_W], o[:, MOBA_W + DIFF_W:],
                   wg, wbr, wo, l1g, l1b, alpha, ms)
        st = state_conv[i]
        zeros = jnp.zeros((bs, ts - 1, dff), F32)
        p1 = jnp.concatenate([st[:, 1:2], zeros], axis=1).reshape(ms, dff)
        p2 = jnp.concatenate([st, zeros[:, 1:]], axis=1).reshape(ms, dff)
        ys, a_full = _ffn(x1, p_sample[i].reshape(ms, -1), (p1, p2), wu, cw, cb, wd, l2g, l2b, wpl, wpg,
                          alpha, ms, ts)
        outs["cs"].append(a_full.reshape(bs, ts, dff)[:, ts - (CONV_W - 1):, :])
    st = lambda name: jnp.stack(outs[name])
    k_p, v_p = (a.reshape(depth, bp, tp, -1) for a in kv_p)
    k_s, v_s = (a.reshape(depth, bs, ts, -1) for a in kv_s)
    return (yp.reshape(bp, tp, d), ys.reshape(bs, ts, d), k_p, v_p, st("cp"), k_s, v_s, st("cs"))
```

```python
import functools
import math

import numpy as np
import jax
import jax.numpy as jnp
from jax import lax
from jax.experimental import pallas as pl
from jax.experimental.pallas import tpu as pltpu

F32 = jnp.float32
BF16 = jnp.bfloat16

HEAD_DIM = 64
MOBA_HEADS = 4
MOBA_BLOCK = 256
MOBA_TOPK = 3
DIFF_HEADS = 4
SB_HEADS = 4
MOBA_W = MOBA_HEADS * HEAD_DIM
DIFF_W = DIFF_HEADS * 2 * HEAD_DIM
SB_W = SB_HEADS * HEAD_DIM
QK_W = MOBA_W + DIFF_W + SB_W
N_GROUPS = QK_W // HEAD_DIM
N_BRANCH = 3
N_ALIBI = MOBA_HEADS + DIFF_HEADS
CONV_W = 3
LN_EPS = 1e-5
QK_SCALE = HEAD_DIM ** -0.5
NEG = -1e30
LANES = 128
SUBLANES = 8
MXU_TILE = 256
VMEM_LIMIT = 56 * 1024 * 1024
TQ = MOBA_BLOCK
GATE_ROWS = 2 * SUBLANES
DECODE_BLOCKS_PER_STEP = 4
DECODE_SLOTS = 3
SEQS_PER_STEP = 2
SB_DEAD_LOG2 = 160.0
DIFF_HEADS_PER_CHAIN = 2
ONES_ROWS = 2 * SUBLANES
LOG2E = math.log2(math.e)

_SLOPES = [2.0 ** (-8.0 * h / N_ALIBI) for h in range(1, N_ALIBI + 1)]
MOBA_SLOPES = _SLOPES[0::2]
DIFF_SLOPES = _SLOPES[1::2]


def _nt(a, b):
    return lax.dot_general(a, b, (((1,), (1,)), ((), ())), preferred_element_type=F32)


def _nn(a, b):
    return jnp.dot(a, b, preferred_element_type=F32)


def _split(x):
    hi = x.astype(BF16)
    lo = (x - hi.astype(F32)).astype(BF16)
    return hi, lo


def _sigmoid(x):
    return 1.0 / (1.0 + jnp.exp(-x))


def _softplus(z):
    return jnp.maximum(z, 0.0) + jnp.log(1.0 + jnp.exp(-jnp.abs(z)))


def _layer_norm(x, g, b):
    mu = jnp.mean(x, axis=-1, keepdims=True)
    xc = x - mu
    var = jnp.mean(xc * xc, axis=-1, keepdims=True)
    return xc * lax.rsqrt(var + LN_EPS) * g + b


def _params(*sem):
    return pltpu.CompilerParams(dimension_semantics=sem, vmem_limit_bytes=VMEM_LIMIT)


def _iota(shape, dim):
    return lax.broadcasted_iota(jnp.int32, shape, dim)


def _proj_kernel(*refs, n_blk, n_carried):
    x_ref, w_ref = refs[:2]
    k_ref, v_ref, qb_ref, kb_ref, vb_ref, qmf_ref, *rest = refs[2 + n_carried:]
    xb = x_ref[...].astype(BF16)
    q = _nn(xb, w_ref[:, 0:QK_W])
    k = _nn(xb, w_ref[:, QK_W:2 * QK_W])
    v = _nn(xb, w_ref[:, 2 * QK_W:3 * QK_W])
    if n_carried:
        k_ref[...] = k
        v_ref[...] = v
    else:
        k_ref[0] = k
        v_ref[0] = v
        for later in range(1, k_ref.shape[0]):
            k_ref[later] = jnp.zeros_like(k)
            v_ref[later] = jnp.zeros_like(v)
    qb_ref[...] = (q * QK_SCALE).astype(BF16)
    kb_ref[...] = k.astype(BF16)
    vb_ref[...] = v.astype(BF16)
    qmf_ref[...] = q[:, :MOBA_W]
    if n_blk:
        kmean_ref = rest[0]
        for j in range(n_blk):
            blk = k[j * MOBA_BLOCK:(j + 1) * MOBA_BLOCK, :MOBA_W]
            kmean_ref[j:j + 1, :] = jnp.sum(blk, axis=0, keepdims=True) * (1.0 / MOBA_BLOCK)


def _proj(x2d, w, tm, with_kmean, layer, depth, kv_all=()):
    m, d = x2d.shape
    n_blk = tm // MOBA_BLOCK if with_kmean else 0
    row = lambda i: (i, 0)
    out_shape = [jax.ShapeDtypeStruct((depth, m, QK_W), F32), jax.ShapeDtypeStruct((depth, m, QK_W), F32),
                 jax.ShapeDtypeStruct((m, QK_W), BF16), jax.ShapeDtypeStruct((m, QK_W), BF16),
                 jax.ShapeDtypeStruct((m, QK_W), BF16), jax.ShapeDtypeStruct((m, MOBA_W), F32)]
    assert bool(kv_all) == (layer > 0)
    kv_spec = (pl.BlockSpec((None, tm, QK_W), lambda i: (layer, i, 0)) if kv_all
               else pl.BlockSpec((depth, tm, QK_W), lambda i: (0, i, 0)))
    out_specs = ([kv_spec] * 2
                 + [pl.BlockSpec((tm, QK_W), row)] * 3 + [pl.BlockSpec((tm, MOBA_W), row)])
    if with_kmean:
        out_shape.append(jax.ShapeDtypeStruct((m // tm, n_blk, MOBA_W), F32))
        out_specs.append(pl.BlockSpec((None, n_blk, MOBA_W), lambda i: (i, 0, 0)))
    in_specs = [pl.BlockSpec((tm, d), row), pl.BlockSpec(w.shape, lambda i: (0, 0))]
    in_specs += [pl.BlockSpec(memory_space=pl.ANY)] * len(kv_all)
    return pl.pallas_call(
        functools.partial(_proj_kernel, n_blk=n_blk, n_carried=len(kv_all)),
        grid=(m // tm,),
        in_specs=in_specs, out_specs=out_specs, out_shape=out_shape,
        input_output_aliases={2 + i: i for i in range(len(kv_all))},
        compiler_params=_params("arbitrary"), name="proj",
    )(x2d, w, *kv_all)


def _fill_vt(v_ref, vt_ref, n_heads, width):
    t = v_ref.shape[0]
    extra = vt_ref.shape[1] - width
    for h in range(n_heads):
        if extra:
            vt_ref[h, width:, :] = jnp.ones((extra, t), BF16)
    for j in range(t // TQ):
        blk = v_ref[j * TQ:(j + 1) * TQ, :].astype(F32).T.astype(BF16)
        for h in range(n_heads):
            vt_ref[h, :width, j * TQ:(j + 1) * TQ] = blk[h * width:(h + 1) * width, :]


def _stack_heads_t(q_bf16, n_heads, width):
    qt = q_bf16.astype(F32).T * LOG2E
    row_head = _iota((n_heads * width, 1), 0) // width
    return jnp.concatenate([jnp.where(row_head == h, qt, 0.0).astype(BF16) for h in range(n_heads)], axis=1)


def _lane_slopes(slopes, n_lanes):
    lane_head = _iota((1, n_lanes), 1) // TQ
    out = jnp.zeros((1, n_lanes), F32)
    for h, s in enumerate(slopes):
        out = jnp.where(lane_head == h, s, out)
    return out


def _softmax_step(s, m, extra):
    cand = jnp.max(s, axis=0, keepdims=True) + extra
    m_new = jnp.maximum(m, cand)
    alpha = jnp.exp2(m - m_new)
    p = jnp.exp2(s - (m_new - extra))
    return p.astype(BF16), alpha, m_new


def _moba_kernel(q_ref, qf_ref, kmean_ref, k_ref, v_ref, o_ref, vt_ref, bias_ref, selb_ref, acc_ref, s_ref, *,
                 n_blk):
    nq = MOBA_HEADS * TQ
    qi = pl.program_id(1)

    slope_row = _lane_slopes(MOBA_SLOPES, nq) * LOG2E

    @pl.when(qi == 0)
    def _():
        _fill_vt(v_ref, vt_ref, MOBA_HEADS, HEAD_DIM)
        bias_ref[...] = _iota((TQ, nq), 0).astype(F32) * slope_row

    qt_all = _stack_heads_t(q_ref[...], MOBA_HEADS, HEAD_DIM)

    qft_hi, qft_lo = _split(qf_ref[...].T)
    km = kmean_ref[...]
    km = jnp.concatenate([km, jnp.zeros((GATE_ROWS - n_blk, MOBA_W), F32)], axis=0)
    lane_head = _iota((1, MOBA_W), 1) // HEAD_DIM
    blk_row = _iota((GATE_ROWS, 1), 0)
    km_hi, km_lo = _split(jnp.concatenate([jnp.where(lane_head == h, km, 0.0) for h in range(MOBA_HEADS)],
                                          axis=0))
    g_all = _nn(km_hi, qft_hi) + _nn(km_hi, qft_lo) + _nn(km_lo, qft_hi)
    for h in range(MOBA_HEADS):
        g = g_all[h * GATE_ROWS:(h + 1) * GATE_ROWS]
        cnt = jnp.zeros((GATE_ROWS, TQ), F32)
        for mb in range(n_blk - 1):
            gm = g[mb:mb + 1, :]
            tie = jnp.where(blk_row > mb, 1.0, 0.0)
            beats = jnp.where(gm > g, 1.0, jnp.where(gm == g, tie, 0.0))
            cnt = cnt + beats * jnp.where(mb < qi, 1.0, 0.0)
        sel = jnp.where(blk_row < qi, jnp.where(cnt < MOBA_TOPK, 0.0, NEG), NEG)
        selb_ref[:, h * TQ:(h + 1) * TQ] = sel

    acc_ref[...] = jnp.zeros_like(acc_ref)

    def scores(n, diag):
        s = _nn(k_ref[pl.ds(pl.multiple_of(n * TQ, TQ), TQ), :], qt_all) + bias_ref[...]
        if diag:
            s = jnp.where(_iota((TQ, nq), 0) <= _iota((TQ, nq), 1) % TQ, s, NEG)
        return s

    def attend(s, n, diag, m):
        start = pl.multiple_of(n * TQ, TQ)
        extra = slope_row * ((n - qi) * TQ).astype(F32)
        if not diag:
            extra = extra + selb_ref[pl.ds(n, 1), :]
        pb, alpha, m = _softmax_step(s, m, extra)
        for h in range(MOBA_HEADS):
            cols = slice(h * TQ, (h + 1) * TQ)
            pv = _nn(vt_ref[h, :, pl.ds(start, TQ)], pb[:, cols])
            acc_ref[h] = alpha[:, cols] * acc_ref[h] + pv
        return m

    def put(slot, n):
        s_ref[slot] = scores(n, False)

    def use(slot, n, m):
        return attend(s_ref[slot], n, False, m)

    put(0, 0)
    m = attend(scores(qi, True), qi, True, jnp.full((1, nq), NEG, F32))

    def body(u, m):
        put(1, 2 * u + 1)
        m = use(0, 2 * u, m)
        put(0, jnp.minimum(2 * u + 2, qi))
        return use(1, 2 * u + 1, m)

    m = lax.fori_loop(0, qi // 2, body, m)

    @pl.when(qi % 2 == 1)
    def _():
        use(0, qi - 1, m)

    outs = []
    for h in range(MOBA_HEADS):
        acc = acc_ref[h]
        outs.append(acc[:HEAD_DIM] * (1.0 / acc[HEAD_DIM:HEAD_DIM + 1]))
    o_ref[...] = jnp.concatenate(outs, axis=0).T.astype(o_ref.dtype)


def _moba(qb, qmf, kmean, kb, vb):
    b, t, _ = qb.shape
    n_blk = t // TQ
    assert n_blk <= GATE_ROWS
    nq = MOBA_HEADS * TQ
    return pl.pallas_call(
        functools.partial(_moba_kernel, n_blk=n_blk),
        grid=(b, n_blk),
        in_specs=[pl.BlockSpec((None, TQ, MOBA_W), lambda i, j: (i, j, 0)),
                  pl.BlockSpec((None, TQ, MOBA_W), lambda i, j: (i, j, 0)),
                  pl.BlockSpec((None, n_blk, MOBA_W), lambda i, j: (i, 0, 0)),
                  pl.BlockSpec((None, t, MOBA_W), lambda i, j: (i, 0, 0)),
                  pl.BlockSpec((None, t, MOBA_W), lambda i, j: (i, 0, 0))],
        out_specs=pl.BlockSpec((None, TQ, MOBA_W), lambda i, j: (i, j, 0)),
        out_shape=jax.ShapeDtypeStruct((b, t, MOBA_W), BF16),
        scratch_shapes=[pltpu.VMEM((MOBA_HEADS, HEAD_DIM + ONES_ROWS, t), BF16), pltpu.VMEM((TQ, nq), F32),
                        pltpu.VMEM((GATE_ROWS, nq), F32),
                        pltpu.VMEM((MOBA_HEADS, HEAD_DIM + ONES_ROWS, TQ), F32),
                        pltpu.VMEM((2, TQ, nq), F32)],
        compiler_params=_params("arbitrary", "arbitrary"), name="moba",
    )(qb, qmf, kmean, kb, vb)


def _suffix_matrix(n, transposed):
    r = _iota((n, n), 0)
    c = _iota((n, n), 1)
    return jnp.where((c > r) if transposed else (r > c), 1.0, 0.0).astype(BF16)


def _sb_kernel(q_ref, k_ref, v_ref, o_ref, vt_ref, acc_ref):
    nq = SB_HEADS * TQ
    nc = q_ref.shape[0]
    qi = pl.program_id(1)

    @pl.when(qi == 0)
    def _():
        for c in range(nc):
            _fill_vt(v_ref.at[c], vt_ref.at[c], SB_HEADS, HEAD_DIM)

    qt_all = [_stack_heads_t(q_ref[c], SB_HEADS, HEAD_DIM) for c in range(nc)]
    r = _iota((TQ + ONES_ROWS, 2 * TQ), 0)
    c = _iota((TQ + ONES_ROWS, 2 * TQ), 1) % TQ
    later_mat = jnp.where(r >= TQ, -1.0, jnp.where(c > r, -1.0, 0.0)).astype(BF16)
    acc_ref[...] = jnp.zeros_like(acc_ref)

    def scores(c, n):
        return _nn(k_ref[c, pl.ds(pl.multiple_of(n * TQ, TQ), TQ), :], qt_all[c])

    def attend(c, z, n, diag, cs):
        start = pl.multiple_of(n * TQ, TQ)
        sp = jnp.maximum(z, 0.0) + jnp.log(1.0 + jnp.exp2(-jnp.abs(z))) * LOG2E
        if diag:
            before = _iota((TQ, nq), 0) < _iota((TQ, nq), 1) % TQ
            sp_keep = jnp.where(before, sp, 0.0)
        else:
            sp_keep = sp
        sums = _nn(later_mat, jnp.concatenate(_split(sp_keep), axis=0))
        a = jnp.exp2((z - sp) + sums[:TQ] + cs)
        if diag:
            a = jnp.where(before, a, 0.0)
        ab = a.astype(BF16)
        for h in range(SB_HEADS):
            acc_ref[c, h] += _nn(vt_ref[c, h, :, pl.ds(start, TQ)], ab[:, h * TQ:(h + 1) * TQ])
        return cs + sums[TQ:TQ + 1]

    def step(n, diag, css):
        return tuple(attend(c, scores(c, n), n, diag, css[c]) for c in range(nc))

    css = step(qi, True, tuple(jnp.zeros((1, nq), F32) for _ in range(nc)))

    def alive(carry):
        i, css = carry
        top = jnp.max(css[0])
        for c in range(1, nc):
            top = jnp.maximum(top, jnp.max(css[c]))
        return jnp.logical_and(i < qi, top > -SB_DEAD_LOG2)

    lax.while_loop(alive, lambda carry: (carry[0] + 1, step(qi - 1 - carry[0], False, carry[1])),
                   (jnp.int32(0), css))
    for c in range(nc):
        o_ref[c] = jnp.concatenate([acc_ref[c, h] for h in range(SB_HEADS)], axis=0).T.astype(o_ref.dtype)


def _sb(qb, kb, vb):
    b, t, _ = qb.shape
    cb = (QK_W - SB_W) // SB_W
    nc = math.gcd(b, SEQS_PER_STEP)
    return pl.pallas_call(
        _sb_kernel,
        grid=(b // nc, t // TQ),
        in_specs=[pl.BlockSpec((nc, TQ, SB_W), lambda i, j: (i, j, cb)),
                  pl.BlockSpec((nc, t, SB_W), lambda i, j: (i, 0, cb)),
                  pl.BlockSpec((nc, t, SB_W), lambda i, j: (i, 0, cb))],
        out_specs=pl.BlockSpec((nc, TQ, SB_W), lambda i, j: (i, j, 0)),
        out_shape=jax.ShapeDtypeStruct((b, t, SB_W), BF16),
        scratch_shapes=[pltpu.VMEM((nc, SB_HEADS, HEAD_DIM, t), BF16),
                        pltpu.VMEM((nc, SB_HEADS, HEAD_DIM, TQ), F32)],
        compiler_params=_params("arbitrary", "arbitrary"), name="sb",
    )(qb, kb, vb)


def _diff_lambda(lam_ref, lam_init):
    dl = lam_ref[...]
    s1 = jnp.sum(dl[0:1] * dl[1:2], axis=1, keepdims=True)
    s2 = jnp.sum(dl[2:3] * dl[3:4], axis=1, keepdims=True)
    return jnp.exp(s1) - jnp.exp(s2) + lam_init


def _diff_kernel(*refs, lam_init):
    hd = 2 * HEAD_DIM
    nh = DIFF_HEADS_PER_CHAIN
    nc = DIFF_HEADS // nh
    nq = nh * 2 * TQ
    q_refs, k_refs, v_refs = refs[:nc], refs[nc:2 * nc], refs[2 * nc:3 * nc]
    lam_ref, g_ref, o_ref, vt_ref, bias_ref, acc_ref, s_ref = refs[3 * nc:]
    qi = pl.program_id(1)
    slope_rows = []
    for c in range(nc):
        lane_head = _iota((1, nq), 1) // (2 * TQ) + c * nh
        row = jnp.zeros((1, nq), F32)
        for i in range(DIFF_HEADS):
            row = jnp.where(lane_head == i, DIFF_SLOPES[i] * LOG2E, row)
        slope_rows.append(row)

    @pl.when(qi == 0)
    def _():
        for c in range(nc):
            _fill_vt(v_refs[c], vt_ref.at[c], nh, hd)
            bias_ref[c] = _iota((TQ, nq), 0).astype(F32) * slope_rows[c]

    qt_all = [_stack_heads_t(q_refs[c][...], 2 * nh, HEAD_DIM) for c in range(nc)]
    acc_ref[...] = jnp.zeros_like(acc_ref)

    def scores(c, n, diag):
        s = _nn(k_refs[c][pl.ds(pl.multiple_of(n * TQ, TQ), TQ), :], qt_all[c]) + bias_ref[c]
        if diag:
            s = jnp.where(_iota((TQ, nq), 0) <= _iota((TQ, nq), 1) % TQ, s, NEG)
        return s

    def attend(c, s, n, m):
        start = pl.multiple_of(n * TQ, TQ)
        extra = slope_rows[c] * ((n - qi) * TQ).astype(F32)
        pb, alpha, m = _softmax_step(s, m, extra)
        for h in range(nh):
            cols = slice(h * 2 * TQ, (h + 1) * 2 * TQ)
            pv = _nn(vt_ref[c, h, :, pl.ds(start, TQ)], pb[:, cols])
            acc_ref[c, h] = alpha[:, cols] * acc_ref[c, h] + pv
        return m

    def put(slot, n):
        for c in range(nc):
            s_ref[slot, c] = scores(c, n, False)

    def use(slot, n, ms):
        return tuple(attend(c, s_ref[slot, c], n, ms[c]) for c in range(nc))

    put(0, 0)
    ms = tuple(attend(c, scores(c, qi, True), qi, jnp.full((1, nq), NEG, F32)) for c in range(nc))

    def body(u, ms):
        put(1, 2 * u + 1)
        ms = use(0, 2 * u, ms)
        put(0, jnp.minimum(2 * u + 2, qi))
        return use(1, 2 * u + 1, ms)

    ms = lax.fori_loop(0, qi // 2, body, ms)

    @pl.when(qi % 2 == 1)
    def _():
        use(0, qi - 1, ms)

    lam = _diff_lambda(lam_ref, lam_init)
    outs = []
    for c in range(nc):
        for h in range(nh):
            acc = acc_ref[c, h]
            acc = acc[:hd] * (1.0 / acc[hd:hd + 1])
            o = acc[:, :TQ] - lam * acc[:, TQ:]
            ms2 = jnp.mean(o * o, axis=0, keepdims=True)
            outs.append(o * lax.rsqrt(ms2 + LN_EPS) * g_ref[...] * (1.0 - lam_init))
    o_ref[...] = jnp.concatenate(outs, axis=0).T.astype(o_ref.dtype)


def _diff(qb, kb, vb, lam_p, subln_g_col, lam_init):
    b, t, _ = qb.shape
    nh = DIFF_HEADS_PER_CHAIN
    nc = DIFF_HEADS // nh
    w = nh * 2 * HEAD_DIM
    assert MOBA_W % w == 0 and DIFF_HEADS % nh == 0
    c0 = MOBA_W // w
    tile = lambda c: pl.BlockSpec((None, TQ, w), lambda i, j: (i, j, c0 + c))
    seq = lambda c: pl.BlockSpec((None, t, w), lambda i, j: (i, 0, c0 + c))
    rows = 2 * HEAD_DIM + ONES_ROWS
    return pl.pallas_call(
        functools.partial(_diff_kernel, lam_init=lam_init),
        grid=(b, t // TQ),
        in_specs=([tile(c) for c in range(nc)] + [seq(c) for c in range(nc)] * 2
                  + [pl.BlockSpec(lam_p.shape, lambda i, j: (0, 0)),
                     pl.BlockSpec(subln_g_col.shape, lambda i, j: (0, 0))]),
        out_specs=pl.BlockSpec((None, TQ, DIFF_W), lambda i, j: (i, j, 0)),
        out_shape=jax.ShapeDtypeStruct((b, t, DIFF_W), BF16),
        scratch_shapes=[pltpu.VMEM((nc, nh, rows, t), BF16),
                        pltpu.VMEM((nc, TQ, nh * 2 * TQ), F32),
                        pltpu.VMEM((nc, nh, rows, 2 * TQ), F32),
                        pltpu.VMEM((2, nc, TQ, nh * 2 * TQ), F32)],
        compiler_params=_params("arbitrary", "arbitrary"), name="diff",
    )(*([qb] * nc), *([kb] * nc), *([vb] * nc), lam_p, subln_g_col)


def _sample_attn_kernel(pt_ref, qbd_ref, qg_ref, kn_ref, vn_ref, lam_ref, g_ref, ck_ref, cv_ref, o_ref,
                        acc_ref, m_ref, l_ref, cs_ref, gall_ref, mall_ref, lall_ref, oall_ref,
                        kbuf_ref, vbuf_ref, sem_ref, *,
                        layer, n_blk, tq, past_len, lam_init, n_pages_step, blocks_per_step):
    rows = tq * N_GROUPS
    b = pl.program_id(0)
    j = pl.program_id(1)
    last = pl.num_programs(1) - 1
    page = kbuf_ref.shape[2]
    ppb = MOBA_BLOCK // page
    n_chunks = n_blk // blocks_per_step
    total_chunks = pl.num_programs(0) * n_chunks

    def chunk_copies(c):
        slot = c % DECODE_SLOTS
        seq = c // n_chunks
        first_page = (n_blk - (c % n_chunks + 1) * blocks_per_step) * ppb
        copies = []
        for p in range(n_pages_step):
            pid = pt_ref[seq, first_page + p]
            copies.append(pltpu.make_async_copy(ck_ref.at[layer, pid], kbuf_ref.at[slot, p],
                                                sem_ref.at[slot, p]))
            copies.append(pltpu.make_async_copy(cv_ref.at[layer, pid], vbuf_ref.at[slot, p],
                                                sem_ref.at[slot, n_pages_step + p]))
        return copies

    def start_chunk(c):
        @pl.when(c < total_chunks)
        def _():
            for cp in chunk_copies(c):
                cp.start()

    @pl.when((b == 0) & (j == 0))
    def _():
        for c in range(DECODE_SLOTS - 1):
            start_chunk(c)

    row = _iota((rows, 1), 0)
    grp = row % N_GROUPS
    qidx = row // N_GROUPS
    kind = jnp.where(grp < MOBA_HEADS, 0, jnp.where(grp < MOBA_HEADS + 2 * DIFF_HEADS, 1, 2))
    is_moba = kind == 0
    is_diff = kind == 1
    is_sb = kind == 2
    slope = jnp.zeros((rows, 1), F32)
    for h in range(MOBA_HEADS):
        slope = jnp.where(grp == h, MOBA_SLOPES[h], slope)
    for h in range(DIFF_HEADS):
        slope = jnp.where(grp - MOBA_HEADS - 2 * h == 0, DIFF_SLOPES[h], slope)
        slope = jnp.where(grp - MOBA_HEADS - 2 * h == 1, DIFF_SLOPES[h], slope)
    blk_lane = _iota((rows, LANES), 1)

    @pl.when(j == 0)
    def _():
        acc_ref[...] = jnp.zeros_like(acc_ref)
        m_ref[...] = jnp.full_like(m_ref, NEG)
        l_ref[...] = jnp.zeros_like(l_ref)
        cs_ref[...] = jnp.zeros_like(cs_ref)
        gall_ref[...] = jnp.zeros_like(gall_ref)
        mall_ref[...] = jnp.full_like(mall_ref, NEG)
        lall_ref[...] = jnp.zeros_like(lall_ref)

    def process(kb, vb, seg, n_first, pos0, limit):
        nk = kb.shape[0]
        n_seg = nk // seg
        s_raw = _nt(qbd_ref[...], kb)
        col = _iota((1, nk), 1)
        s = s_raw + slope * (col.astype(F32) + pos0)
        if limit is not None:
            valid = col < limit
            s = jnp.where(valid, s, NEG)
        segs = [slice(i * seg, (i + 1) * seg) for i in range(n_seg)]
        seg_max = [jnp.max(s[:, sl], axis=1, keepdims=True) for sl in segs]
        m_old = m_ref[...]
        m_new = m_old
        for sm in seg_max:
            m_new = jnp.maximum(m_new, sm)
        alpha = jnp.exp(m_old - m_new)
        sp = _softplus(s_raw)
        lk = -sp
        if limit is not None:
            lk = jnp.where(valid, lk, 0.0)
        later_mat = _suffix_matrix(seg, False)
        run = cs_ref[...]
        a_segs = [None] * n_seg
        for i in reversed(range(n_seg)):
            lk_hi, lk_lo = _split(lk[:, segs[i]])
            later = _nn(lk_hi, later_mat) + _nn(lk_lo, later_mat) + run
            a_segs[i] = jnp.exp((s_raw[:, segs[i]] - sp[:, segs[i]]) + later)
            run = run + jnp.sum(lk[:, segs[i]], axis=1, keepdims=True)
        cs_ref[...] = run
        psum = jnp.zeros((rows, 1), F32)
        p_segs = []
        mall, lall = mall_ref[...], lall_ref[...]
        for i in range(n_seg):
            p = jnp.exp(s[:, segs[i]] - jnp.where(is_moba, seg_max[i], m_new))
            ps = jnp.sum(p, axis=1, keepdims=True)
            psum = psum + ps
            mall = jnp.where(blk_lane == n_first + i, seg_max[i], mall)
            lall = jnp.where(blk_lane == n_first + i, ps, lall)
            a = a_segs[i]
            if limit is not None:
                a = jnp.where(valid[:, segs[i]], a, 0.0)
            p_segs.append(jnp.where(is_sb, a, p).astype(BF16))
        mall_ref[...] = mall
        lall_ref[...] = lall
        l_ref[...] = jnp.where(is_diff, alpha * l_ref[...] + psum, l_ref[...])
        m_ref[...] = jnp.where(is_diff, m_new, m_old)
        for i in range(n_seg):
            oall_ref[n_first + i] = _nn(p_segs[i], vb[segs[i], :MOBA_W])
        pmat = p_segs[0] if n_seg == 1 else jnp.concatenate(p_segs, axis=1)
        fac = jnp.where(is_diff, alpha, jnp.where(is_sb, 1.0, 0.0))
        acc_ref[...] = acc_ref[...] * fac + _nn(pmat, vb[:, MOBA_W:])

    @pl.when(j == 0)
    def _():
        limit = jnp.where(is_sb, qidx, qidx + 1)
        process(kn_ref[...], vn_ref[...], kn_ref.shape[0], n_blk, 0.0, limit)

    @pl.when(j > 0)
    def _():
        chunk = b * n_chunks + j - 1
        for cp in chunk_copies(chunk):
            cp.wait()
        start_chunk(chunk + DECODE_SLOTS - 1)
        slot = chunk % DECODE_SLOTS
        n_first = n_blk - j * blocks_per_step
        gall = gall_ref[...]
        for i in range(blocks_per_step):
            ksum = jnp.zeros((1, MOBA_W), F32)
            for p in range(ppb):
                ksum = ksum + jnp.sum(kbuf_ref[slot, i * ppb + p, :, :MOBA_W], axis=0, keepdims=True)
            gate = jnp.sum(qg_ref[...] * (ksum * (1.0 / MOBA_BLOCK)), axis=1, keepdims=True)
            gall = jnp.where(blk_lane == n_first + i, gate, gall)
        gall_ref[...] = gall
        kb = jnp.concatenate([kbuf_ref[slot, p].astype(BF16) for p in range(n_pages_step)], axis=0)
        vb = jnp.concatenate([vbuf_ref[slot, p].astype(BF16) for p in range(n_pages_step)], axis=0)
        pos0 = (n_first * MOBA_BLOCK - past_len).astype(F32)
        process(kb, vb, MOBA_BLOCK, n_first, pos0, None)

    @pl.when(j == last)
    def _():
        g = gall_ref[...]
        cnt = jnp.zeros((rows, LANES), F32)
        for mb in range(n_blk):
            gm = g[:, mb:mb + 1]
            tie = jnp.where(blk_lane > mb, 1.0, 0.0)
            cnt = cnt + jnp.where(gm > g, 1.0, jnp.where(gm == g, tie, 0.0))
        rank_lim = jnp.where(blk_lane < n_blk, float(MOBA_TOPK), jnp.where(blk_lane == n_blk, 1e9, -1.0))
        sel = cnt < rank_lim
        mall = jnp.where(sel, mall_ref[...], NEG)
        m_tot = jnp.max(mall, axis=1, keepdims=True)
        w = jnp.where(sel, jnp.exp(mall - m_tot), 0.0)
        l_tot = jnp.sum(w * lall_ref[...], axis=1, keepdims=True)
        o_m = jnp.zeros((rows, MOBA_W), F32)
        for nb in range(n_blk + 1):
            o_m = o_m + w[:, nb:nb + 1] * oall_ref[nb]
        o_m = o_m * jnp.where(is_moba, 1.0 / l_tot, 0.0)
        rest = acc_ref[...] * jnp.where(is_diff, 1.0 / l_ref[...], jnp.where(is_sb, 1.0, 0.0))
        acc = jnp.concatenate([o_m, rest], axis=1)
        lam = _diff_lambda(lam_ref, lam_init)
        g16 = _iota((N_GROUPS, 1), 0)
        cgrp = _iota((1, QK_W), 1) // HEAD_DIM
        d_lo, d_hi = MOBA_HEADS, MOBA_HEADS + 2 * DIFF_HEADS
        g_diff = jnp.where(g16 >= d_lo, jnp.where(g16 < d_hi, 1, 0), 0)
        c_diff = jnp.where(cgrp >= d_lo, jnp.where(cgrp < d_hi, 1, 0), 0)
        g_key = jnp.where(g_diff == 1, d_lo + 2 * ((g16 - d_lo) // 2), g16)
        c_key = jnp.where(c_diff == 1, d_lo + 2 * ((cgrp - d_lo) // 2), cgrp)
        second_half = g_diff * ((g16 - d_lo) % 2)
        coef = jnp.where(g_key == c_key, jnp.where(second_half == 1, -lam, 1.0), 0.0)
        hd = 2 * HEAD_DIM
        gsub = g_ref[...]
        for t in range(tq):
            o_t = jnp.sum(acc[t * N_GROUPS:(t + 1) * N_GROUPS, :] * coef, axis=0, keepdims=True)
            pieces = [o_t[:, :MOBA_W]]
            for h in range(DIFF_HEADS):
                od = o_t[:, MOBA_W + h * hd:MOBA_W + (h + 1) * hd]
                ms2 = jnp.mean(od * od, axis=1, keepdims=True)
                pieces.append(od * lax.rsqrt(ms2 + LN_EPS) * gsub * (1.0 - lam_init))
            pieces.append(o_t[:, MOBA_W + DIFF_W:])
            o_ref[t:t + 1, :] = jnp.concatenate(pieces, axis=1)


def _sample_attn(layer, qb, qmf, kb_new, vb_new, cache_k, cache_v, page_table, lam_p, subln_g, lam_init):
    b, tq, _ = qb.shape
    page = cache_k.shape[2]
    n_pages = page_table.shape[1]
    past_len = n_pages * page
    assert past_len % MOBA_BLOCK == 0 and MOBA_BLOCK % page == 0 and tq <= SUBLANES
    n_blk = past_len // MOBA_BLOCK
    assert n_blk + 1 <= LANES
    bps = math.gcd(n_blk, DECODE_BLOCKS_PER_STEP)
    pps = bps * (MOBA_BLOCK // page)
    rows = tq * N_GROUPS
    n_new = LANES

    gmask = (np.arange(QK_W)[None, :] // HEAD_DIM == np.arange(N_GROUPS)[:, None])
    qbd = jnp.where(gmask[None, None], qb[:, :, None, :], jnp.zeros((), BF16)).reshape(b, rows, QK_W)
    qg = jnp.where(gmask[None, None, :, :MOBA_W], qmf[:, :, None, :], 0.0).reshape(b, rows, MOBA_W)
    pad = ((0, 0), (0, n_new - tq), (0, 0))
    kn = jnp.pad(kb_new, pad)
    vn = jnp.pad(vb_new, pad)

    per_b = lambda i, j, pt: (i, 0, 0)
    whole = lambda i, j, pt: (0, 0)
    in_specs = [pl.BlockSpec((None, rows, QK_W), per_b),
                pl.BlockSpec((None, rows, MOBA_W), per_b),
                pl.BlockSpec((None, n_new, QK_W), per_b),
                pl.BlockSpec((None, n_new, QK_W), per_b),
                pl.BlockSpec(lam_p.shape, whole),
                pl.BlockSpec(subln_g.shape, whole),
                pl.BlockSpec(memory_space=pl.ANY),
                pl.BlockSpec(memory_space=pl.ANY)]
    grid_spec = pltpu.PrefetchScalarGridSpec(
        num_scalar_prefetch=1, grid=(b, n_blk // bps + 1), in_specs=in_specs,
        out_specs=pl.BlockSpec((None, tq, QK_W), per_b),
        scratch_shapes=[pltpu.VMEM((rows, QK_W - MOBA_W), F32), pltpu.VMEM((rows, 1), F32),
                        pltpu.VMEM((rows, 1), F32), pltpu.VMEM((rows, 1), F32),
                        pltpu.VMEM((rows, LANES), F32), pltpu.VMEM((rows, LANES), F32),
                        pltpu.VMEM((rows, LANES), F32), pltpu.VMEM((n_blk + 1, rows, MOBA_W), F32),
                        pltpu.VMEM((DECODE_SLOTS, pps, page, QK_W), F32),
                        pltpu.VMEM((DECODE_SLOTS, pps, page, QK_W), F32),
                        pltpu.SemaphoreType.DMA((DECODE_SLOTS, 2 * pps))])
    return pl.pallas_call(
        functools.partial(_sample_attn_kernel, layer=layer, n_blk=n_blk, tq=tq, past_len=past_len,
                          lam_init=lam_init, n_pages_step=pps, blocks_per_step=bps),
        grid_spec=grid_spec,
        out_shape=jax.ShapeDtypeStruct((b, tq, QK_W), F32),
        compiler_params=_params("arbitrary", "arbitrary"), name="sample_attn",
    )(page_table, qbd, qg, kn, vn, lam_p, subln_g, cache_k, cache_v)


def _row_chains(tm):
    half = tm // 2
    if half % MOBA_BLOCK:
        return [slice(0, tm)]
    return [slice(0, half), slice(half, tm)]


def _post_kernel(x_ref, om_ref, od_ref, os_ref, wg_ref, wb_ref, wo_ref, g_ref, b_ref, y_ref, *, alpha):
    tm, d = x_ref.shape
    bounds = (0, MOBA_W, MOBA_W + DIFF_W, MOBA_W + DIFF_W + SB_W)
    for rows in _row_chains(tm):
        x = x_ref[rows, :]
        xb = x.astype(BF16)
        merged = jnp.zeros(x.shape, F32)
        for i, o_ref in enumerate((om_ref, od_ref, os_ref)):
            gate = _sigmoid(_nn(xb, wg_ref[:, i * d:(i + 1) * d]))
            merged = merged + gate * _nn(o_ref[rows, :], wb_ref[bounds[i]:bounds[i + 1], :])
        mix = _nn(merged.astype(BF16), wo_ref[...])
        y_ref[rows, :] = _layer_norm(alpha * x + mix, g_ref[...], b_ref[...])


def _post(x2d, o_m, o_d, o_s, w_gate, w_branch, w_out, ln_g, ln_b, alpha, tm):
    m, d = x2d.shape
    row = lambda i: (i, 0)
    full = lambda a: pl.BlockSpec(a.shape, lambda i: (0, 0))
    return pl.pallas_call(
        functools.partial(_post_kernel, alpha=alpha),
        grid=(m // tm,),
        in_specs=[pl.BlockSpec((tm, d), row), pl.BlockSpec((tm, MOBA_W), row),
                  pl.BlockSpec((tm, DIFF_W), row), pl.BlockSpec((tm, SB_W), row),
                  full(w_gate), full(w_branch), full(w_out), full(ln_g), full(ln_b)],
        out_specs=pl.BlockSpec((tm, d), row),
        out_shape=jax.ShapeDtypeStruct((m, d), F32),
        compiler_params=_params("arbitrary"), name="post",
    )(x2d, o_m, o_d, o_s, w_gate, w_branch, w_out, ln_g, ln_b)


def _gelu(x):
    return 0.5 * x * (1.0 + lax.erf(x * (2.0 ** -0.5)))


FFN_CHUNKS = 2


def _ffn_chunks(dff):
    unit = MXU_TILE if dff % MXU_TILE == 0 else LANES
    tiles = dff // unit
    assert tiles * unit == dff
    bounds = [unit * ((tiles * i) // FFN_CHUNKS) for i in range(FFN_CHUNKS + 1)]
    return [(bounds[i], bounds[i + 1]) for i in range(FFN_CHUNKS) if bounds[i + 1] > bounds[i]]


def _ffn_kernel(*refs, alpha, tm, seq, has_past):
    if has_past:
        x_ref, ple_ref, p1_ref, p2_ref = refs[:4]
        refs = refs[4:]
    else:
        x_ref, ple_ref = refs[:2]
        refs = refs[2:]
    (wup_ref, cw_ref, cb_ref, wdn_ref, g_ref, b_ref, wple_ref, wpg_ref,
     y_ref, conv_ref, aext_ref) = refs
    dff = cw_ref.shape[1]
    ti = pl.program_id(1)

    @pl.when(ti == 0)
    def _():
        aext_ref[0:SUBLANES, :] = jnp.zeros((SUBLANES, dff), F32)

    x = x_ref[...]
    xb = x.astype(BF16)
    ffn = jnp.zeros(x.shape, F32)
    for c0, c1 in _ffn_chunks(dff):
        cs = slice(c0, c1)
        a = _nn(xb, wup_ref[:, c0:c1])
        gate_in = _nn(xb, wup_ref[:, dff + c0:dff + c1])
        aext_ref[SUBLANES:SUBLANES + tm, cs] = a
        a1 = aext_ref[SUBLANES - 1:SUBLANES - 1 + tm, cs]
        a2 = aext_ref[SUBLANES - 2:SUBLANES - 2 + tm, cs]
        if has_past:
            t = _iota((tm, 1), 0) % seq
            a1 = jnp.where(t >= 1, a1, p1_ref[:, cs])
            a2 = jnp.where(t >= 2, a2, p2_ref[:, cs])
            conv_ref[:, cs] = a
        else:
            aext_ref[0:SUBLANES, cs] = a[tm - SUBLANES:, :]
            conv_ref[:, cs] = a[tm - SUBLANES:, :]
        a_conv = cb_ref[:, cs] + (cw_ref[0:1, cs] * a2 + cw_ref[1:2, cs] * a1 + cw_ref[2:3, cs] * a)
        hid = _gelu(a_conv) * gate_in
        ffn = ffn + _nn(hid.astype(BF16), wdn_ref[c0:c1, :])
    y = _layer_norm(alpha * x + ffn, g_ref[...], b_ref[...])
    pg = _sigmoid(_nn(y.astype(BF16), wpg_ref[...]))
    y_ref[...] = y + pg * _nn(ple_ref[...].astype(BF16), wple_ref[...])


def _ffn(x2d, ple_all, layer, past, w_up, conv_w, conv_b, w_down, ln_g, ln_b, w_ple, w_pg, alpha, tm, seq):
    m, d = x2d.shape
    dff = conv_w.shape[1]
    has_past = past is not None
    if has_past:
        assert m == tm and tm % seq == 0
        grid = (1, 1)
        nb = 1
    else:
        assert seq % tm == 0 and tm >= SUBLANES
        nb = seq // tm
        grid = (m // seq, nb)
    row = lambda i, j: (i * nb + j, 0)
    full = lambda a: pl.BlockSpec(a.shape, lambda i, j: (0,) * a.ndim)
    in_specs = [pl.BlockSpec((tm, d), row),
                pl.BlockSpec((None, tm, ple_all.shape[2]), lambda i, j: (layer, i * nb + j, 0))]
    args = [x2d, ple_all]
    if has_past:
        in_specs += [pl.BlockSpec((tm, dff), row)] * 2
        args += list(past)
        conv_shape = jax.ShapeDtypeStruct((m, dff), F32)
        conv_spec = pl.BlockSpec((tm, dff), row)
    else:
        conv_shape = jax.ShapeDtypeStruct((m // seq, SUBLANES, dff), F32)
        conv_spec = pl.BlockSpec((None, SUBLANES, dff), lambda i, j: (i, 0, 0))
    weights = [w_up, conv_w, conv_b, w_down, ln_g, ln_b, w_ple, w_pg]
    in_specs += [full(w) for w in weights]
    return pl.pallas_call(
        functools.partial(_ffn_kernel, alpha=alpha, tm=tm, seq=seq, has_past=has_past),
        grid=grid, in_specs=in_specs,
        out_specs=[pl.BlockSpec((tm, d), row), conv_spec],
        out_shape=[jax.ShapeDtypeStruct((m, d), F32), conv_shape],
        scratch_shapes=[pltpu.VMEM((tm + SUBLANES, dff), F32)],
        compiler_params=_params("arbitrary", "arbitrary"), name="ffn",
    )(*args, *weights)


def kernel(x_prompt, x_sample, cache_k, cache_v, state_conv, page_table, p_prompt, p_sample, w_in, w_gate, w_branch, w_out, diff_lambda, diff_subln_g, ln1_g, ln1_b, w_up, conv_w, conv_b, w_down, ln2_g, ln2_b, w_ple, w_ple_gate):
    depth = w_in.shape[0]
    alpha = (2.0 * depth) ** 0.25
    bp, tp, d = x_prompt.shape
    bs, ts, _ = x_sample.shape
    dff = conv_w.shape[2]
    tm_p = 512 if tp % 512 == 0 else MOBA_BLOCK
    tm_f = 512 if tp % 512 == 0 else MOBA_BLOCK
    assert tp % MOBA_BLOCK == 0
    yp = x_prompt.reshape(bp * tp, d)
    ys = x_sample.reshape(bs * ts, d)
    outs = {k: [] for k in ("cp", "cs")}
    kv_p, kv_s = (), ()
    row2 = lambda a: a.reshape(1, -1)
    for i in range(depth):
        lam_init = 0.8 - 0.6 * math.exp(-0.3 * i)
        wi, wg, wbr, wo = (w[i].astype(BF16) for w in (w_in, w_gate, w_branch, w_out))
        wu, wd, wpl, wpg = (w[i].astype(BF16) for w in (w_up, w_down, w_ple, w_ple_gate))
        lam_p, subg = diff_lambda[i], row2(diff_subln_g[i])
        l1g, l1b, l2g, l2b, cb = (row2(a[i]) for a in (ln1_g, ln1_b, ln2_g, ln2_b, conv_b))
        cw = conv_w[i]

        *kv_p, qb, kb, vb, qmf, kmean = _proj(yp, wi, tm_p, True, i, depth, tuple(kv_p))
        to3 = lambda a: a.reshape(bp, tp, a.shape[-1])
        qb3, kb3, vb3 = to3(qb), to3(kb), to3(vb)
        o_m = _moba(qb3, to3(qmf), kmean.reshape(bp, tp // MOBA_BLOCK, MOBA_W), kb3, vb3)
        o_d = _diff(qb3, kb3, vb3, lam_p, subg.reshape(-1, 1), lam_init)
        o_s = _sb(qb3, kb3, vb3)
        x1 = _post(yp, o_m.reshape(-1, MOBA_W), o_d.reshape(-1, DIFF_W), o_s.reshape(-1, SB_W),
                   wg, wbr, wo, l1g, l1b, alpha, tm_f)
        yp, conv = _ffn(x1, p_prompt.reshape(depth, bp * tp, -1), i, None, wu, cw, cb, wd, l2g, l2b, wpl, wpg,
                        alpha, tm_f, tp)
        outs["cp"].append(conv[:, SUBLANES - (CONV_W - 1):, :])

        ms = bs * ts
        *kv_s, qb, kb, vb, qmf = _proj(ys, wi, ms, False, i, depth, tuple(kv_s))
        to3 = lambda a: a.reshape(bs, ts, a.shape[-1])
        o = _sample_attn(i, to3(qb), to3(qmf), to3(kb), to3(vb), cache_k, cache_v, page_table,
                         lam_p, subg, lam_init).reshape(ms, -1).astype(BF16)
        x1 = _post(ys, o[:, :MOBA_W], o[:, MOBA_W:MOBA_W + DIFF_W], o[:, MOBA_W + DIFF_W:],
                   wg, wbr, wo, l1g, l1b, alpha, ms)
        st = state_conv[i]
        zeros = jnp.zeros((bs, ts - 1, dff), F32)
        p1 = jnp.concatenate([st[:, 1:2], zeros], axis=1).reshape(ms, dff)
        p2 = jnp.concatenate([st, zeros[:, 1:]], axis=1).reshape(ms, dff)
        ys, a_full = _ffn(x1, p_sample.reshape(depth, ms, -1), i, (p1, p2), wu, cw, cb, wd, l2g, l2b, wpl, wpg,
                          alpha, ms, ts)
        outs["cs"].append(a_full.reshape(bs, ts, dff)[:, ts - (CONV_W - 1):, :])
    st = lambda name: jnp.stack(outs[name])
    k_p, v_p = (a.reshape(depth, bp, tp, -1) for a in kv_p)
    k_s, v_s = (a.reshape(depth, bs, ts, -1) for a in kv_s)
    return (yp.reshape(bp, tp, d), ys.reshape(bs, ts, d), k_p, v_p, st("cp"), k_s, v_s, st("cs"))
```
